```python
import jax, jax.numpy as jnp
from jax import lax
import numpy as np

D_MODEL = 2048
BATCH = 4
SEQ = 4096
DEPTH = 1

PLE_DIM = 256
D_FF = 4 * D_MODEL
EPS = 1e-6
GLA_HEADS = 4
GLA_KEY_W = D_MODEL // 4
GLA_VAL_W = D_MODEL // 2
GLA_DK = GLA_KEY_W // GLA_HEADS
GLA_DV = GLA_VAL_W // GLA_HEADS
GATE_RANK = 16
GATE_TAU = 16.0
CHUNK = 64
SB_W = D_MODEL // 2
SB_DH = 128
SB_HEADS = SB_W // SB_DH
SB_BLOCK = 128
IN_SPLITS = (GLA_KEY_W, GLA_KEY_W, GLA_VAL_W, GATE_RANK, GLA_VAL_W,
             SB_W, SB_W, SB_W, D_MODEL, D_MODEL)
IN_W = sum(IN_SPLITS)

kernel_name = "hybrid_gla_stickbreaking_gated_block"


def rmsnorm(h, gain):
    hf = h.astype(jnp.float32)
    hf = hf * lax.rsqrt(jnp.mean(hf * hf, axis=-1, keepdims=True) + EPS)
    return (hf * gain.astype(jnp.float32)).astype(h.dtype)


def split_cols(z, sizes):
    offs = np.cumsum(sizes)[:-1].tolist()
    return jnp.split(z, offs, axis=-1)


def gla_chunked(q, k, v, log_a):
    B, T, H, dk = q.shape
    dv = v.shape[-1]
    N = T // CHUNK

    def chunk(z):
        return z.astype(jnp.float32).reshape(B, N, CHUNK, H, z.shape[-1]).transpose(1, 0, 3, 2, 4)

    qc, kc, vc, lac = chunk(q), chunk(k), chunk(v), chunk(log_a)
    b = jnp.cumsum(lac, axis=3)
    b_last = b[:, :, :, -1:, :]
    q_dec = qc * (dk ** -0.5) * jnp.exp(b)
    k_intra = kc * jnp.exp(-b)
    k_state = kc * jnp.exp(b_last - b)
    decay = jnp.exp(b_last[:, :, :, 0, :])

    causal = jnp.tril(jnp.ones((CHUNK, CHUNK), dtype=bool))
    scores = jnp.einsum('nbhcd,nbhsd->nbhcs', q_dec, k_intra)
    scores = jnp.where(causal, scores, 0.0)
    o_intra = jnp.einsum('nbhcs,nbhse->nbhce', scores, vc)

    def step(S, inp):
        qd, ks, vv, dec = inp
        o = jnp.einsum('bhcd,bhde->bhce', qd, S)
        S = dec[..., None] * S + jnp.einsum('bhcd,bhce->bhde', ks, vv)
        return S, o

    S0 = jnp.zeros((B, H, dk, dv), jnp.float32)
    _, o_inter = lax.scan(step, S0, (q_dec, k_state, vc, decay))
    o = o_intra + o_inter
    return o.transpose(1, 0, 3, 2, 4).reshape(B, T, H, dv)


def stick_breaking_attention(q, k, v):
    T = q.shape[2]
    d = q.shape[-1]
    scale = d ** -0.5
    qf, kf, vf = q.astype(jnp.float32), k.astype(jnp.float32), v.astype(jnp.float32)
    outs = []
    for i in range(T // SB_BLOCK):
        q0, kend = i * SB_BLOCK, (i + 1) * SB_BLOCK
        qb = qf[:, :, q0:kend]
        kb, vb = kf[:, :, :kend], vf[:, :, :kend]
        z = jnp.einsum('bhqd,bhkd->bhqk', qb, kb) * scale
        t_idx = q0 + jnp.arange(SB_BLOCK)
        s_idx = jnp.arange(kend)
        mask = s_idx[None, :] < t_idx[:, None]
        log_beta = jax.nn.log_sigmoid(z)
        log_1mb = jnp.where(mask, jax.nn.log_sigmoid(-z), 0.0)
        suffix = lax.cumsum(log_1mb, axis=3, reverse=True) - log_1mb
        A = jnp.where(mask, jnp.exp(log_beta + suffix), 0.0)
        outs.append(jnp.einsum('bhqk,bhkd->bhqd', A, vb))
    return jnp.concatenate(outs, axis=2)


def mixer_block(h, w_in, w_gate_up, b_gate, gla_norm, w_branch_gla, w_branch_sb, w_out):
    B, T, _ = h.shape
    z = h @ w_in
    (gq, gk, gv, g_lr, g_out, sq, sk, sv, gate_a, gate_b) = split_cols(z, IN_SPLITS)

    log_a = jax.nn.log_sigmoid((g_lr @ w_gate_up + b_gate).astype(jnp.float32)) / GATE_TAU
    o_a = gla_chunked(gq.reshape(B, T, GLA_HEADS, GLA_DK),
                      gk.reshape(B, T, GLA_HEADS, GLA_DK),
                      gv.reshape(B, T, GLA_HEADS, GLA_DV),
                      log_a.reshape(B, T, GLA_HEADS, GLA_DK))
    o_a = rmsnorm(o_a, gla_norm).reshape(B, T, GLA_VAL_W).astype(h.dtype)
    o_a = o_a * jax.nn.silu(g_out)
    y_a = o_a @ w_branch_gla

    def heads(t):
        return t.reshape(B, T, SB_HEADS, SB_DH).transpose(0, 2, 1, 3)
    o_b = stick_breaking_attention(heads(sq), heads(sk), heads(sv))
    o_b = o_b.transpose(0, 2, 1, 3).reshape(B, T, SB_W).astype(h.dtype)
    y_b = o_b @ w_branch_sb

    y = jax.nn.sigmoid(gate_a) * y_a + jax.nn.sigmoid(gate_b) * y_b
    return y @ w_out


def setup_inputs(seed: int = 0) -> dict:
    key = jax.random.key(seed)
    ks = jax.random.split(key, 20)

    def w(k, shape, fan_in):
        return jax.random.normal(k, shape, jnp.float32) * (fan_in ** -0.5)

    def gain(k, n):
        return 1.0 + 0.02 * jax.random.normal(k, (DEPTH, n), jnp.float32)

    return {
        "x": jax.random.normal(ks[0], (BATCH, SEQ, D_MODEL), jnp.float32),
        "p": jax.random.normal(ks[1], (DEPTH, BATCH, SEQ, PLE_DIM), jnp.float32),
        "norm_mix_pre": gain(ks[2], D_MODEL),
        "norm_mix_post": gain(ks[3], D_MODEL),
        "w_in": w(ks[4], (DEPTH, D_MODEL, IN_W), D_MODEL),
        "w_gate_up": w(ks[5], (DEPTH, GATE_RANK, GLA_KEY_W), GATE_RANK),
        "b_gate": 0.1 * jax.random.normal(ks[6], (DEPTH, GLA_KEY_W), jnp.float32),
        "gla_norm": gain(ks[7], GLA_DV),
        "w_branch_gla": w(ks[8], (DEPTH, GLA_VAL_W, D_MODEL), GLA_VAL_W),
        "w_branch_sb": w(ks[9], (DEPTH, SB_W, D_MODEL), SB_W),
        "w_out": w(ks[10], (DEPTH, D_MODEL, D_MODEL), D_MODEL),
        "norm_mlp_pre": gain(ks[11], D_MODEL),
        "norm_mlp_post": gain(ks[12], D_MODEL),
        "w_mlp_up": w(ks[13], (DEPTH, D_MODEL, D_FF), D_MODEL),
        "w_mlp_down": w(ks[14], (DEPTH, D_FF, D_MODEL), D_FF),
        "norm_ple": gain(ks[15], D_MODEL),
        "w_ple_gate": w(ks[16], (DEPTH, D_MODEL, D_MODEL), D_MODEL),
        "w_ple_proj": w(ks[17], (DEPTH, PLE_DIM, D_MODEL), PLE_DIM),
    }


def reference(x, p, norm_mix_pre, norm_mix_post, w_in, w_gate_up, b_gate, gla_norm,
              w_branch_gla, w_branch_sb, w_out, norm_mlp_pre, norm_mlp_post,
              w_mlp_up, w_mlp_down, norm_ple, w_ple_gate, w_ple_proj):
    h = x
    for i in range(DEPTH):
        u = rmsnorm(h, norm_mix_pre[i])
        m = mixer_block(u, w_in[i], w_gate_up[i], b_gate[i], gla_norm[i],
                        w_branch_gla[i], w_branch_sb[i], w_out[i])
        h = h + rmsnorm(m, norm_mix_post[i])
        u = rmsnorm(h, norm_mlp_pre[i])
        f = jnp.square(jax.nn.relu(u @ w_mlp_up[i])) @ w_mlp_down[i]
        h = h + rmsnorm(f, norm_mlp_post[i])
        e = p[i] @ w_ple_proj[i]
        g = jax.nn.sigmoid(rmsnorm(h, norm_ple[i]) @ w_ple_gate[i])
        h = h + g * e
    return h
```

```python
import functools

import jax
import jax.numpy as jnp
from jax import lax
from jax.experimental import pallas as pl
from jax.experimental.pallas import tpu as pltpu

F32 = jnp.float32
BF16 = jnp.bfloat16

EPS = 1e-6
GLA_HEADS = 4
GLA_DK = 128
GLA_DV = 256
GATE_RANK = 16
GATE_TAU = 16.0
GLA_CHUNK = 64
SB_DH = 128
LANE = 128

_KEY_W = GLA_HEADS * GLA_DK
_VAL_W = GLA_HEADS * GLA_DV

VMEM_LIMIT = 56 * 1024 * 1024


def _rms(x, gain):
    ms = jnp.mean(x * x, axis=-1, keepdims=True)
    return x * lax.rsqrt(ms + EPS) * gain


def _sigmoid(x):
    return 1.0 / (1.0 + jnp.exp(-x))


def _softplus(x):
    return jnp.maximum(x, 0.0) + jnp.log(1.0 + jnp.exp(-jnp.abs(x)))


def _split_bf16(x):
    hi = x.astype(BF16)
    lo = (x - hi.astype(F32)).astype(BF16)
    return hi, lo


def _dot(a, b):
    return jnp.dot(a, b, preferred_element_type=F32)


def _dot_nt(a, b):
    return lax.dot_general(a, b, (((1,), (1,)), ((), ())), preferred_element_type=F32)


def _dot_tn(a, b):
    return lax.dot_general(a, b, (((0,), (0,)), ((), ())), preferred_element_type=F32)


def _in_proj_kernel(x_ref, g_ref, w_ref, wlr_ref, z_ref, zlr_ref, u_ref):
    @pl.when(pl.program_id(1) == 0)
    def _():
        u = _rms(x_ref[...], g_ref[...]).astype(BF16)
        u_ref[...] = u
        zlr_ref[...] = _dot(u, wlr_ref[...]).astype(zlr_ref.dtype)

    z_ref[...] = _dot(u_ref[...], w_ref[...]).astype(z_ref.dtype)


def _in_proj(x2, gain, w_main, w_lr, *, tm, tn):
    m, d = x2.shape
    n = w_main.shape[1]
    return pl.pallas_call(
        _in_proj_kernel,
        grid=(m // tm, n // tn),
        in_specs=[
            pl.BlockSpec((tm, d), lambda i, j: (i, 0)),
            pl.BlockSpec((1, d), lambda i, j: (0, 0)),
            pl.BlockSpec((d, tn), lambda i, j: (0, j)),
            pl.BlockSpec((d, LANE), lambda i, j: (0, 0)),
        ],
        out_specs=[
            pl.BlockSpec((tm, tn), lambda i, j: (i, j)),
            pl.BlockSpec((tm, LANE), lambda i, j: (i, 0)),
        ],
        out_shape=[
            jax.ShapeDtypeStruct((m, n), BF16),
            jax.ShapeDtypeStruct((m, LANE), BF16),
        ],
        scratch_shapes=[pltpu.VMEM((tm, d), BF16)],
        compiler_params=pltpu.CompilerParams(
            dimension_semantics=("parallel", "arbitrary"),
            vmem_limit_bytes=VMEM_LIMIT),
        name="in_proj",
    )(x2, gain, w_main, w_lr)


def _gla_kernel(q_ref, k_ref, v_ref, go_ref, lr_ref, wgu_ref, bg_ref, gn_ref,
                o_ref, s_ref, *, rows):
    @pl.when(pl.program_id(1) == 0)
    def _():
        s_ref[...] = jnp.zeros_like(s_ref)

    c = GLA_CHUNK
    r_idx = lax.broadcasted_iota(jnp.int32, (c, c), 0)
    c_idx = lax.broadcasted_iota(jnp.int32, (c, c), 1)
    causal = r_idx >= c_idx
    tril = causal.astype(BF16)
    scale = GLA_DK ** -0.5
    gn = gn_ref[...]

    def chunk(ci, carry):
        rs = pl.ds(pl.multiple_of(ci * c, c), c)
        pre = _dot(lr_ref[rs, :], wgu_ref[...]) + bg_ref[...]
        la = (jnp.minimum(pre, 0.0) - jnp.log(1.0 + jnp.exp(-jnp.abs(pre)))) * (1.0 / GATE_TAU)
        la_hi, la_lo = _split_bf16(la)
        b = _dot(tril, la_hi) + _dot(tril, la_lo)
        b_last = b[c - 1:c, :]
        q = q_ref[rs, :].astype(F32)
        k = k_ref[rs, :].astype(F32)
        q_dec = (q * scale * jnp.exp(b)).astype(BF16)
        k_intra = (k * jnp.exp(-b)).astype(BF16)
        k_state = (k * jnp.exp(b_last - b)).astype(BF16)
        decay = jnp.exp(b_last)
        for h in range(GLA_HEADS):
            ks = slice(h * GLA_DK, (h + 1) * GLA_DK)
            vs = slice(h * GLA_DV, (h + 1) * GLA_DV)
            v_h = v_ref[rs, vs]
            scores = _dot_nt(q_dec[:, ks], k_intra[:, ks])
            scores = jnp.where(causal, scores, 0.0).astype(BF16)
            st = s_ref[h]
            o = _dot(scores, v_h) + _dot_nt(q_dec[:, ks], st.astype(BF16))
            s_ref[h] = decay[:, ks] * st + _dot_tn(v_h, k_state[:, ks])
            o = _rms(o, gn)
            go = go_ref[rs, vs].astype(F32)
            o_ref[rs, vs] = (o * (go * _sigmoid(go))).astype(o_ref.dtype)
        return carry

    lax.fori_loop(0, rows // c, chunk, 0)


def _gla(z, z_lr, w_gu, b_gate, gla_norm, *, batch, seq, rows):
    m = z.shape[0]
    nb = seq // rows
    kw, vw = _KEY_W, _VAL_W
    row = lambda b, r: b * nb + r
    return pl.pallas_call(
        functools.partial(_gla_kernel, rows=rows),
        grid=(batch, nb),
        in_specs=[
            pl.BlockSpec((rows, kw), lambda b, r: (row(b, r), 0)),
            pl.BlockSpec((rows, kw), lambda b, r: (row(b, r), 1)),
            pl.BlockSpec((rows, vw), lambda b, r: (row(b, r), 1)),
            pl.BlockSpec((rows, vw), lambda b, r: (row(b, r), 2)),
            pl.BlockSpec((rows, LANE), lambda b, r: (row(b, r), 0)),
            pl.BlockSpec((LANE, kw), lambda b, r: (0, 0)),
            pl.BlockSpec((1, kw), lambda b, r: (0, 0)),
            pl.BlockSpec((1, GLA_DV), lambda b, r: (0, 0)),
        ],
        out_specs=pl.BlockSpec((rows, vw), lambda b, r: (row(b, r), 0)),
        out_shape=jax.ShapeDtypeStruct((m, vw), BF16),
        scratch_shapes=[pltpu.VMEM((GLA_HEADS, GLA_DV, GLA_DK), F32)],
        compiler_params=pltpu.CompilerParams(
            dimension_semantics=("parallel", "arbitrary"),
            vmem_limit_bytes=VMEM_LIMIT),
        name="gla",
    )(z, z, z, z, z_lr, w_gu, b_gate, gla_norm)


def _sb_kernel(q_ref, k_ref, v_ref, o_ref, *, tq, tk):
    q0 = pl.program_id(2) * tq
    q = q_ref[...]
    scale = SB_DH ** -0.5
    j_idx = lax.broadcasted_iota(jnp.int32, (tk, tk), 0)
    s_idx = lax.broadcasted_iota(jnp.int32, (tk, tk), 1)
    suffix_ones = (j_idx >= s_idx).astype(BF16)
    t_loc = lax.broadcasted_iota(jnp.int32, (tq, tk), 0)
    s_loc = lax.broadcasted_iota(jnp.int32, (tq, tk), 1)

    def tile(kb, carry, acc, masked):
        k0 = pl.multiple_of(kb * tk, tk)
        kblk = k_ref[pl.ds(k0, tk), :]
        vblk = v_ref[pl.ds(k0, tk), :]
        z = _dot_nt(q, kblk) * scale
        sp = _softplus(z)
        if masked:
            mask = (s_loc + k0) < (t_loc + q0)
            sp = jnp.where(mask, sp, 0.0)
        hi, lo = _split_bf16(sp)
        cs = _dot(hi, suffix_ones) + _dot(lo, suffix_ones)
        a = jnp.exp(z - (cs + carry))
        if masked:
            a = jnp.where(mask, a, 0.0)
        acc = acc + _dot(a.astype(BF16), vblk)
        return carry + cs[:, 0:1], acc

    carry = jnp.zeros((tq, 1), F32)
    acc = jnp.zeros((tq, SB_DH), F32)
    n_diag = tq // tk
    kb_top = (q0 + tq) // tk - 1
    for d in range(n_diag):
        carry, acc = tile(kb_top - d, carry, acc, True)

    def body(i, state):
        carry, acc = state
        return tile(kb_top - n_diag - i, carry, acc, False)

    carry, acc = lax.fori_loop(0, q0 // tk, body, (carry, acc))
    o_ref[...] = acc.astype(o_ref.dtype)


def _sb_attn(z, *, batch, seq, heads, q_off, k_off, v_off, tq, tk):
    m = z.shape[0]
    nq = seq // tq
    return pl.pallas_call(
        functools.partial(_sb_kernel, tq=tq, tk=tk),
        grid=(batch, heads, nq),
        in_specs=[
            pl.BlockSpec((tq, SB_DH), lambda b, h, i: (b * nq + i, q_off + h)),
            pl.BlockSpec((seq, SB_DH), lambda b, h, i: (b, k_off + h)),
            pl.BlockSpec((seq, SB_DH), lambda b, h, i: (b, v_off + h)),
        ],
        out_specs=pl.BlockSpec((tq, SB_DH), lambda b, h, i: (b * nq + i, h)),
        out_shape=jax.ShapeDtypeStruct((m, heads * SB_DH), BF16),
        compiler_params=pltpu.CompilerParams(
            dimension_semantics=("parallel", "parallel", "arbitrary"),
            vmem_limit_bytes=VMEM_LIMIT),
        name="sb_attn",
    )(z, z, z)


def _mix_out_kernel(oa_ref, ob_ref, ga_ref, gb_ref, x_ref, wa_ref, wb_ref, wo_ref, g_ref, h_ref):
    ya = _dot(oa_ref[...], wa_ref[...])
    yb = _dot(ob_ref[...], wb_ref[...])
    y = _sigmoid(ga_ref[...].astype(F32)) * ya + _sigmoid(gb_ref[...].astype(F32)) * yb
    mix = _dot(y.astype(BF16), wo_ref[...])
    h_ref[...] = x_ref[...] + _rms(mix, g_ref[...])


def _resident(shape):
    return pl.BlockSpec(shape, lambda *_: (0,) * len(shape), pipeline_mode=pl.Buffered(1))


def _mix_out(o_a, o_b, z, x2, w_a, w_b, w_o, gain, *, ga_blk, gb_blk, tm):
    m, d = x2.shape
    return pl.pallas_call(
        _mix_out_kernel,
        grid=(m // tm,),
        in_specs=[
            pl.BlockSpec((tm, o_a.shape[1]), lambda i: (i, 0)),
            pl.BlockSpec((tm, o_b.shape[1]), lambda i: (i, 0)),
            pl.BlockSpec((tm, d), lambda i: (i, ga_blk)),
            pl.BlockSpec((tm, d), lambda i: (i, gb_blk)),
            pl.BlockSpec((tm, d), lambda i: (i, 0)),
            _resident(w_a.shape),
            _resident(w_b.shape),
            _resident(w_o.shape),
            _resident(gain.shape),
        ],
        out_specs=pl.BlockSpec((tm, d), lambda i: (i, 0)),
        out_shape=jax.ShapeDtypeStruct((m, d), F32),
        compiler_params=pltpu.CompilerParams(
            dimension_semantics=("parallel",),
            vmem_limit_bytes=VMEM_LIMIT),
        name="mix_out",
    )(o_a, o_b, z, z, x2, w_a, w_b, w_o, gain)


def _mlp_kernel(h_ref, gpre_ref, wu_ref, wd_ref, gpost_ref, o_ref, u_ref, acc_ref):
    f = pl.program_id(1)

    @pl.when(f == 0)
    def _():
        u_ref[...] = _rms(h_ref[...], gpre_ref[...]).astype(BF16)
        acc_ref[...] = jnp.zeros_like(acc_ref)

    a = jnp.maximum(_dot(u_ref[...], wu_ref[...]), 0.0)
    acc_ref[...] += _dot((a * a).astype(BF16), wd_ref[...])

    @pl.when(f == pl.num_programs(1) - 1)
    def _():
        o_ref[...] = h_ref[...] + _rms(acc_ref[...], gpost_ref[...])


def _mlp(h, g_pre, w_up, w_down, g_post, *, tm, tf):
    m, d = h.shape
    ff = w_up.shape[1]
    return pl.pallas_call(
        _mlp_kernel,
        grid=(m // tm, ff // tf),
        in_specs=[
            pl.BlockSpec((tm, d), lambda i, f: (i, 0)),
            pl.BlockSpec((1, d), lambda i, f: (0, 0)),
            pl.BlockSpec((d, tf), lambda i, f: (0, f)),
            pl.BlockSpec((tf, d), lambda i, f: (f, 0)),
            pl.BlockSpec((1, d), lambda i, f: (0, 0)),
        ],
        out_specs=pl.BlockSpec((tm, d), lambda i, f: (i, 0)),
        out_shape=jax.ShapeDtypeStruct((m, d), F32),
        scratch_shapes=[pltpu.VMEM((tm, d), BF16), pltpu.VMEM((tm, d), F32)],
        compiler_params=pltpu.CompilerParams(
            dimension_semantics=("parallel", "arbitrary"),
            vmem_limit_bytes=VMEM_LIMIT),
        name="mlp",
    )(h, g_pre, w_up, w_down, g_post)


def _ple_kernel(h_ref, p_ref, g_ref, wg_ref, wp_ref, o_ref):
    h = h_ref[...]
    e = _dot(p_ref[...].astype(BF16), wp_ref[...])
    gate = _sigmoid(_dot(_rms(h, g_ref[...]).astype(BF16), wg_ref[...]))
    o_ref[...] = h + gate * e


def _ple(h, p2, gain, w_gate, w_proj, *, tm):
    m, d = h.shape
    return pl.pallas_call(
        _ple_kernel,
        grid=(m // tm,),
        in_specs=[
            pl.BlockSpec((tm, d), lambda i: (i, 0)),
            pl.BlockSpec((tm, p2.shape[1]), lambda i: (i, 0)),
            _resident(gain.shape),
            _resident(w_gate.shape),
            _resident(w_proj.shape),
        ],
        out_specs=pl.BlockSpec((tm, d), lambda i: (i, 0)),
        out_shape=jax.ShapeDtypeStruct((m, d), F32),
        compiler_params=pltpu.CompilerParams(
            dimension_semantics=("parallel",),
            vmem_limit_bytes=VMEM_LIMIT),
        name="ple",
    )(h, p2, gain, w_gate, w_proj)


def _regroup_w_in(w_in, d_model):
    kw, vw = _KEY_W, _VAL_W
    sb_w = d_model // 2
    sizes = (kw, kw, vw, GATE_RANK, vw, sb_w, sb_w, sb_w, d_model, d_model)
    offs = [0]
    for s in sizes:
        offs.append(offs[-1] + s)
    seg = lambda i: w_in[:, offs[i]:offs[i + 1]]
    w_main = jnp.concatenate([seg(i) for i in (0, 1, 2, 4, 5, 6, 7, 8, 9)], axis=1).astype(BF16)
    w_lr = jnp.pad(seg(3), ((0, 0), (0, LANE - GATE_RANK))).astype(BF16)
    return w_main, w_lr


def kernel(x, p, norm_mix_pre, norm_mix_post, w_in, w_gate_up, b_gate, gla_norm, w_branch_gla,
           w_branch_sb, w_out, norm_mlp_pre, norm_mlp_post, w_mlp_up, w_mlp_down, norm_ple,
           w_ple_gate, w_ple_proj):
    batch, seq, d = x.shape
    depth = w_in.shape[0]
    m = batch * seq
    sb_w = d // 2
    sb_heads = sb_w // SB_DH
    h = x.reshape(m, d)
    row = lambda g: g.reshape(1, -1)
    for i in range(depth):
        w_main, w_lr = _regroup_w_in(w_in[i], d)
        w_gu = jnp.pad(w_gate_up[i], ((0, LANE - GATE_RANK), (0, 0))).astype(BF16)
        z, z_lr = _in_proj(h, row(norm_mix_pre[i]), w_main, w_lr, tm=1024, tn=1024)
        o_a = _gla(z, z_lr, w_gu, row(b_gate[i]), row(gla_norm[i]), batch=batch, seq=seq, rows=512)
        sq_off = (2 * _KEY_W + 2 * _VAL_W) // SB_DH
        o_b = _sb_attn(z, batch=batch, seq=seq, heads=sb_heads, q_off=sq_off,
                       k_off=sq_off + sb_heads, v_off=sq_off + 2 * sb_heads, tq=256, tk=128)
        gate_blk = (2 * _KEY_W + 2 * _VAL_W + 3 * sb_w) // d
        h = _mix_out(o_a, o_b, z, h, w_branch_gla[i].astype(BF16), w_branch_sb[i].astype(BF16),
                     w_out[i].astype(BF16), row(norm_mix_post[i]),
                     ga_blk=gate_blk, gb_blk=gate_blk + 1, tm=512)
        h = _mlp(h, row(norm_mlp_pre[i]), w_mlp_up[i].astype(BF16), w_mlp_down[i].astype(BF16),
                 row(norm_mlp_post[i]), tm=512, tf=1024)
        h = _ple(h, p[i].reshape(m, -1), row(norm_ple[i]), w_ple_gate[i].astype(BF16),
                 w_ple_proj[i].astype(BF16), tm=512)
    return h.reshape(batch, seq, d)
```

```python
import functools

import jax
import jax.numpy as jnp
from jax import lax
from jax.experimental import pallas as pl
from jax.experimental.pallas import tpu as pltpu

F32 = jnp.float32
BF16 = jnp.bfloat16

EPS = 1e-6
GLA_HEADS = 4
GLA_DK = 128
GLA_DV = 256
GATE_RANK = 16
GATE_TAU = 16.0
GLA_CHUNK = 64
SB_DH = 128
LANE = 128

_KEY_W = GLA_HEADS * GLA_DK
_VAL_W = GLA_HEADS * GLA_DV

VMEM_LIMIT = 56 * 1024 * 1024


def _rms(x, gain):
    ms = jnp.mean(x * x, axis=-1, keepdims=True)
    return x * lax.rsqrt(ms + EPS) * gain


def _sigmoid(x):
    return 1.0 / (1.0 + jnp.exp(-x))


def _softplus(x):
    return jnp.maximum(x, 0.0) + jnp.log(1.0 + jnp.exp(-jnp.abs(x)))


def _split_bf16(x):
    hi = x.astype(BF16)
    lo = (x - hi.astype(F32)).astype(BF16)
    return hi, lo


def _dot(a, b):
    return jnp.dot(a, b, preferred_element_type=F32)


def _dot_nt(a, b):
    return lax.dot_general(a, b, (((1,), (1,)), ((), ())), preferred_element_type=F32)


def _dot_tn(a, b):
    return lax.dot_general(a, b, (((0,), (0,)), ((), ())), preferred_element_type=F32)


def _in_proj_kernel(x_ref, g_ref, w_ref, wlr_ref, z_ref, zlr_ref, u_ref):
    @pl.when(pl.program_id(1) == 0)
    def _():
        u = _rms(x_ref[...], g_ref[...]).astype(BF16)
        u_ref[...] = u
        zlr_ref[...] = _dot(u, wlr_ref[...]).astype(zlr_ref.dtype)

    z_ref[...] = _dot(u_ref[...], w_ref[...]).astype(z_ref.dtype)


def _in_proj(x2, gain, w_main, w_lr, *, tm, tn):
    m, d = x2.shape
    n = w_main.shape[1]
    return pl.pallas_call(
        _in_proj_kernel,
        grid=(m // tm, n // tn),
        in_specs=[
            pl.BlockSpec((tm, d), lambda i, j: (i, 0)),
            pl.BlockSpec((1, d), lambda i, j: (0, 0)),
            pl.BlockSpec((d, tn), lambda i, j: (0, j)),
            pl.BlockSpec((d, LANE), lambda i, j: (0, 0)),
        ],
        out_specs=[
            pl.BlockSpec((tm, tn), lambda i, j: (i, j)),
            pl.BlockSpec((tm, LANE), lambda i, j: (i, 0)),
        ],
        out_shape=[
            jax.ShapeDtypeStruct((m, n), BF16),
            jax.ShapeDtypeStruct((m, LANE), BF16),
        ],
        scratch_shapes=[pltpu.VMEM((tm, d), BF16)],
        compiler_params=pltpu.CompilerParams(
            dimension_semantics=("parallel", "arbitrary"),
            vmem_limit_bytes=VMEM_LIMIT),
        name="in_proj",
    )(x2, gain, w_main, w_lr)


def _gla_kernel(q_ref, k_ref, v_ref, go_ref, lr_ref, wgu_ref, bg_ref, gn_ref,
                o_ref, s_ref, *, rows):
    @pl.when(pl.program_id(1) == 0)
    def _():
        s_ref[...] = jnp.zeros_like(s_ref)

    c = GLA_CHUNK
    r_idx = lax.broadcasted_iota(jnp.int32, (c, c), 0)
    c_idx = lax.broadcasted_iota(jnp.int32, (c, c), 1)
    causal = r_idx >= c_idx
    tril = causal.astype(BF16)
    scale = GLA_DK ** -0.5
    gn = gn_ref[...]

    def chunk(ci, carry):
        rs = pl.ds(pl.multiple_of(ci * c, c), c)
        pre = _dot(lr_ref[rs, :], wgu_ref[...]) + bg_ref[...]
        la = (jnp.minimum(pre, 0.0) - jnp.log(1.0 + jnp.exp(-jnp.abs(pre)))) * (1.0 / GATE_TAU)
        la_hi, la_lo = _split_bf16(la)
        b = _dot(tril, la_hi) + _dot(tril, la_lo)
        b_last = b[c - 1:c, :]
        q = q_ref[rs, :].astype(F32)
        k = k_ref[rs, :].astype(F32)
        q_dec = (q * scale * jnp.exp(b)).astype(BF16)
        k_intra = (k * jnp.exp(-b)).astype(BF16)
        k_state = (k * jnp.exp(b_last - b)).astype(BF16)
        decay = jnp.exp(b_last)
        for h in range(GLA_HEADS):
            ks = slice(h * GLA_DK, (h + 1) * GLA_DK)
            vs = slice(h * GLA_DV, (h + 1) * GLA_DV)
            v_h = v_ref[rs, vs]
            scores = _dot_nt(q_dec[:, ks], k_intra[:, ks])
            scores = jnp.where(causal, scores, 0.0).astype(BF16)
            st = s_ref[h]
            o = _dot(scores, v_h) + _dot_nt(q_dec[:, ks], st.astype(BF16))
            s_ref[h] = decay[:, ks] * st + _dot_tn(v_h, k_state[:, ks])
            o = _rms(o, gn)
            go = go_ref[rs, vs].astype(F32)
            o_ref[rs, vs] = (o * (go * _sigmoid(go))).astype(o_ref.dtype)
        return carry

    lax.fori_loop(0, rows // c, chunk, 0)


def _gla(z, z_lr, w_gu, b_gate, gla_norm, *, batch, seq, rows):
    m = z.shape[0]
    nb = seq // rows
    kw, vw = _KEY_W, _VAL_W
    row = lambda b, r: b * nb + r
    return pl.pallas_call(
        functools.partial(_gla_kernel, rows=rows),
        grid=(batch, nb),
        in_specs=[
            pl.BlockSpec((rows, kw), lambda b, r: (row(b, r), 0)),
            pl.BlockSpec((rows, kw), lambda b, r: (row(b, r), 1)),
            pl.BlockSpec((rows, vw), lambda b, r: (row(b, r), 1)),
            pl.BlockSpec((rows, vw), lambda b, r: (row(b, r), 2)),
            pl.BlockSpec((rows, LANE), lambda b, r: (row(b, r), 0)),
            pl.BlockSpec((LANE, kw), lambda b, r: (0, 0)),
            pl.BlockSpec((1, kw), lambda b, r: (0, 0)),
            pl.BlockSpec((1, GLA_DV), lambda b, r: (0, 0)),
        ],
        out_specs=pl.BlockSpec((rows, vw), lambda b, r: (row(b, r), 0)),
        out_shape=jax.ShapeDtypeStruct((m, vw), BF16),
        scratch_shapes=[pltpu.VMEM((GLA_HEADS, GLA_DV, GLA_DK), F32)],
        compiler_params=pltpu.CompilerParams(
            dimension_semantics=("parallel", "arbitrary"),
            vmem_limit_bytes=VMEM_LIMIT),
        name="gla",
    )(z, z, z, z, z_lr, w_gu, b_gate, gla_norm)


SB_SKIP = 105.0


def _sb_kernel(q_ref, k_ref, v_ref, o_ref, acc_ref, carry_ref, *, tq, tk, heads):
    q0 = pl.program_id(1) * tq
    scale = SB_DH ** -0.5
    j_idx = lax.broadcasted_iota(jnp.int32, (tk, tk), 0)
    s_idx = lax.broadcasted_iota(jnp.int32, (tk, tk), 1)
    suffix_ones = (j_idx >= s_idx).astype(BF16)
    suffix_ones2 = jnp.concatenate([suffix_ones, suffix_ones], axis=0)
    t_loc = lax.broadcasted_iota(jnp.int32, (tq, tk), 0)
    s_loc = lax.broadcasted_iota(jnp.int32, (tq, tk), 1)

    def tile(h, k0, mask, first):
        hs = slice(h * SB_DH, (h + 1) * SB_DH)
        kblk = k_ref[pl.ds(k0, tk), hs]
        vblk = v_ref[pl.ds(k0, tk), hs]
        z = _dot_nt(q_ref[:, hs], kblk) * scale
        sp = _softplus(z)
        if mask is not None:
            sp = jnp.where(mask, sp, 0.0)
        hi, lo = _split_bf16(sp)
        cs = _dot(jnp.concatenate([hi, lo], axis=1), suffix_ones2)
        carry = None if first else carry_ref[h]
        a = jnp.exp(z - (cs if first else cs + carry))
        if mask is not None:
            a = jnp.where(mask, a, 0.0)
        pv = _dot(a.astype(BF16), vblk)
        new_carry = cs[:, 0:1] if first else carry + cs[:, 0:1]
        carry_ref[h] = new_carry
        if first:
            acc_ref[:, hs] = pv
        else:
            acc_ref[:, hs] += pv
        return new_carry

    def carry_min(parts):
        return jnp.min(functools.reduce(jnp.minimum, parts))

    n_diag = tq // tk
    parts = None
    for d in range(n_diag):
        k0 = pl.multiple_of(q0 + (n_diag - 1 - d) * tk, tk)
        mask = (s_loc + k0) < (t_loc + q0)
        parts = [tile(h, k0, mask, d == 0) for h in range(heads)]

    def cond(state):
        kb, cmin = state
        return jnp.logical_and(kb >= 0, cmin < SB_SKIP)

    def body(state):
        kb, _ = state
        k0 = pl.multiple_of(kb * tk, tk)
        parts = [tile(h, k0, None, False) for h in range(heads)]
        return kb - 1, carry_min(parts)

    lax.while_loop(cond, body, (q0 // tk - 1, carry_min(parts)))
    o_ref[...] = acc_ref[...].astype(o_ref.dtype)


def _sb_attn(z, *, batch, seq, heads, q_off, k_off, v_off, tq, tk):
    m = z.shape[0]
    nq = seq // tq
    w = heads * SB_DH
    return pl.pallas_call(
        functools.partial(_sb_kernel, tq=tq, tk=tk, heads=heads),
        grid=(batch, nq),
        in_specs=[
            pl.BlockSpec((tq, w), lambda b, i: (b * nq + i, q_off)),
            pl.BlockSpec((seq, w), lambda b, i: (b, k_off)),
            pl.BlockSpec((seq, w), lambda b, i: (b, v_off)),
        ],
        out_specs=pl.BlockSpec((tq, w), lambda b, i: (b * nq + i, 0)),
        out_shape=jax.ShapeDtypeStruct((m, w), BF16),
        scratch_shapes=[pltpu.VMEM((tq, w), F32), pltpu.VMEM((heads, tq, 1), F32)],
        compiler_params=pltpu.CompilerParams(
            dimension_semantics=("parallel", "arbitrary"),
            vmem_limit_bytes=VMEM_LIMIT),
        name="sb_attn",
    )(z, z, z)


def _mix_out_kernel(oa_ref, ob_ref, ga_ref, gb_ref, x_ref, wa_ref, wb_ref, wo_ref, g_ref, h_ref):
    ya = _dot(oa_ref[...], wa_ref[...])
    yb = _dot(ob_ref[...], wb_ref[...])
    y = _sigmoid(ga_ref[...].astype(F32)) * ya + _sigmoid(gb_ref[...].astype(F32)) * yb
    mix = _dot(y.astype(BF16), wo_ref[...])
    h_ref[...] = x_ref[...] + _rms(mix, g_ref[...])


def _resident(shape):
    return pl.BlockSpec(shape, lambda *_: (0,) * len(shape), pipeline_mode=pl.Buffered(1))


def _mix_out(o_a, o_b, z, x2, w_a, w_b, w_o, gain, *, ga_blk, gb_blk, tm):
    m, d = x2.shape
    return pl.pallas_call(
        _mix_out_kernel,
        grid=(m // tm,),
        in_specs=[
            pl.BlockSpec((tm, o_a.shape[1]), lambda i: (i, 0)),
            pl.BlockSpec((tm, o_b.shape[1]), lambda i: (i, 0)),
            pl.BlockSpec((tm, d), lambda i: (i, ga_blk)),
            pl.BlockSpec((tm, d), lambda i: (i, gb_blk)),
            pl.BlockSpec((tm, d), lambda i: (i, 0)),
            _resident(w_a.shape),
            _resident(w_b.shape),
            _resident(w_o.shape),
            _resident(gain.shape),
        ],
        out_specs=pl.BlockSpec((tm, d), lambda i: (i, 0)),
        out_shape=jax.ShapeDtypeStruct((m, d), F32),
        compiler_params=pltpu.CompilerParams(
            dimension_semantics=("parallel",),
            vmem_limit_bytes=VMEM_LIMIT),
        name="mix_out",
    )(o_a, o_b, z, z, x2, w_a, w_b, w_o, gain)


def _mlp_kernel(h_ref, gpre_ref, wu_ref, wd_ref, gpost_ref, o_ref, u_ref, acc_ref):
    f = pl.program_id(1)

    @pl.when(f == 0)
    def _():
        u_ref[...] = _rms(h_ref[...], gpre_ref[...]).astype(BF16)
        acc_ref[...] = jnp.zeros_like(acc_ref)

    a = jnp.maximum(_dot(u_ref[...], wu_ref[...]), 0.0)
    acc_ref[...] += _dot((a * a).astype(BF16), wd_ref[...])

    @pl.when(f == pl.num_programs(1) - 1)
    def _():
        o_ref[...] = h_ref[...] + _rms(acc_ref[...], gpost_ref[...])


def _mlp(h, g_pre, w_up, w_down, g_post, *, tm, tf):
    m, d = h.shape
    ff = w_up.shape[1]
    return pl.pallas_call(
        _mlp_kernel,
        grid=(m // tm, ff // tf),
        in_specs=[
            pl.BlockSpec((tm, d), lambda i, f: (i, 0)),
            pl.BlockSpec((1, d), lambda i, f: (0, 0)),
            pl.BlockSpec((d, tf), lambda i, f: (0, f)),
            pl.BlockSpec((tf, d), lambda i, f: (f, 0)),
            pl.BlockSpec((1, d), lambda i, f: (0, 0)),
        ],
        out_specs=pl.BlockSpec((tm, d), lambda i, f: (i, 0)),
        out_shape=jax.ShapeDtypeStruct((m, d), F32),
        scratch_shapes=[pltpu.VMEM((tm, d), BF16), pltpu.VMEM((tm, d), F32)],
        compiler_params=pltpu.CompilerParams(
            dimension_semantics=("parallel", "arbitrary"),
            vmem_limit_bytes=VMEM_LIMIT),
        name="mlp",
    )(h, g_pre, w_up, w_down, g_post)


def _ple_kernel(h_ref, p_ref, g_ref, wg_ref, wp_ref, o_ref):
    h = h_ref[...]
    e = _dot(p_ref[...].astype(BF16), wp_ref[...])
    gate = _sigmoid(_dot(_rms(h, g_ref[...]).astype(BF16), wg_ref[...]))
    o_ref[...] = h + gate * e


def _ple(h, p2, gain, w_gate, w_proj, *, tm):
    m, d = h.shape
    return pl.pallas_call(
        _ple_kernel,
        grid=(m // tm,),
        in_specs=[
            pl.BlockSpec((tm, d), lambda i: (i, 0)),
            pl.BlockSpec((tm, p2.shape[1]), lambda i: (i, 0)),
            _resident(gain.shape),
            _resident(w_gate.shape),
            _resident(w_proj.shape),
        ],
        out_specs=pl.BlockSpec((tm, d), lambda i: (i, 0)),
        out_shape=jax.ShapeDtypeStruct((m, d), F32),
        compiler_params=pltpu.CompilerParams(
            dimension_semantics=("parallel",),
            vmem_limit_bytes=VMEM_LIMIT),
        name="ple",
    )(h, p2, gain, w_gate, w_proj)


def _regroup_w_in(w_in, d_model):
    kw, vw = _KEY_W, _VAL_W
    sb_w = d_model // 2
    sizes = (kw, kw, vw, GATE_RANK, vw, sb_w, sb_w, sb_w, d_model, d_model)
    offs = [0]
    for s in sizes:
        offs.append(offs[-1] + s)
    seg = lambda i: w_in[:, offs[i]:offs[i + 1]]
    w_main = jnp.concatenate([seg(i) for i in (0, 1, 2, 4, 5, 6, 7, 8, 9)], axis=1).astype(BF16)
    w_lr = jnp.pad(seg(3), ((0, 0), (0, LANE - GATE_RANK))).astype(BF16)
    return w_main, w_lr


def kernel(x, p, norm_mix_pre, norm_mix_post, w_in, w_gate_up, b_gate, gla_norm, w_branch_gla,
           w_branch_sb, w_out, norm_mlp_pre, norm_mlp_post, w_mlp_up, w_mlp_down, norm_ple,
           w_ple_gate, w_ple_proj):
    batch, seq, d = x.shape
    depth = w_in.shape[0]
    m = batch * seq
    sb_w = d // 2
    sb_heads = sb_w // SB_DH
    h = x.reshape(m, d)
    row = lambda g: g.reshape(1, -1)
    for i in range(depth):
        w_main, w_lr = _regroup_w_in(w_in[i], d)
        w_gu = jnp.pad(w_gate_up[i], ((0, LANE - GATE_RANK), (0, 0))).astype(BF16)
        z, z_lr = _in_proj(h, row(norm_mix_pre[i]), w_main, w_lr, tm=1024, tn=1024)
        o_a = _gla(z, z_lr, w_gu, row(b_gate[i]), row(gla_norm[i]), batch=batch, seq=seq, rows=512)
        sq_off = (2 * _KEY_W + 2 * _VAL_W) // sb_w
        o_b = _sb_attn(z, batch=batch, seq=seq, heads=sb_heads, q_off=sq_off,
                       k_off=sq_off + 1, v_off=sq_off + 2, tq=256, tk=256)
        gate_blk = (2 * _KEY_W + 2 * _VAL_W + 3 * sb_w) // d
        h = _mix_out(o_a, o_b, z, h, w_branch_gla[i].astype(BF16), w_branch_sb[i].astype(BF16),
                     w_out[i].astype(BF16), row(norm_mix_post[i]),
                     ga_blk=gate_blk, gb_blk=gate_blk + 1, tm=512)
        h = _mlp(h, row(norm_mlp_pre[i]), w_mlp_up[i].astype(BF16), w_mlp_down[i].astype(BF16),
                 row(norm_mlp_post[i]), tm=512, tf=1024)
        h = _ple(h, p[i].reshape(m, -1), row(norm_ple[i]), w_ple_gate[i].astype(BF16),
                 w_ple_proj[i].astype(BF16), tm=512)
    return h.reshape(batch, seq, d)
```

```python
import functools

import jax
import jax.numpy as jnp
from jax import lax
from jax.experimental import pallas as pl
from jax.experimental.pallas import tpu as pltpu

F32 = jnp.float32
BF16 = jnp.bfloat16

EPS = 1e-6
GLA_HEADS = 4
GLA_DK = 128
GLA_DV = 256
GATE_RANK = 16
GATE_TAU = 16.0
GLA_CHUNK = 64
SB_DH = 128
LANE = 128

_KEY_W = GLA_HEADS * GLA_DK
_VAL_W = GLA_HEADS * GLA_DV

VMEM_LIMIT = 56 * 1024 * 1024


def _rms(x, gain):
    ms = jnp.mean(x * x, axis=-1, keepdims=True)
    return x * lax.rsqrt(ms + EPS) * gain


def _sigmoid(x):
    return 1.0 / (1.0 + jnp.exp(-x))


def _split_bf16(x):
    hi = x.astype(BF16)
    lo = (x - hi.astype(F32)).astype(BF16)
    return hi, lo


def _dot(a, b):
    return jnp.dot(a, b, preferred_element_type=F32)


def _dot_nt(a, b):
    return lax.dot_general(a, b, (((1,), (1,)), ((), ())), preferred_element_type=F32)


def _dot_tn(a, b):
    return lax.dot_general(a, b, (((0,), (0,)), ((), ())), preferred_element_type=F32)


def _in_proj_kernel(x_ref, g_ref, wa_ref, wb_ref, wlr_ref, z_ref, zlr_ref, u_ref, *, na):
    j = pl.program_id(1)

    @pl.when(j == 0)
    def _():
        u = _rms(x_ref[...], g_ref[...]).astype(BF16)
        u_ref[...] = u
        zlr_ref[...] = _dot(u, wlr_ref[...]).astype(zlr_ref.dtype)

    @pl.when(j < na)
    def _():
        z_ref[...] = _dot(u_ref[...], wa_ref[...]).astype(z_ref.dtype)

    @pl.when(j >= na)
    def _():
        z_ref[...] = _dot(u_ref[...], wb_ref[...]).astype(z_ref.dtype)


def _in_proj(x2, gain, w_a, w_b, w_lr, *, tm, tn):
    m, d = x2.shape
    na = w_a.shape[1] // tn
    n = w_a.shape[1] + w_b.shape[1]
    return pl.pallas_call(
        functools.partial(_in_proj_kernel, na=na),
        grid=(m // tm, n // tn),
        in_specs=[
            pl.BlockSpec((tm, d), lambda i, j: (i, 0)),
            pl.BlockSpec((1, d), lambda i, j: (0, 0)),
            pl.BlockSpec((d, tn), lambda i, j: (0, jnp.minimum(j, na - 1))),
            pl.BlockSpec((d, tn), lambda i, j: (0, jnp.maximum(j - na, 0))),
            pl.BlockSpec((d, LANE), lambda i, j: (0, 0)),
        ],
        out_specs=[
            pl.BlockSpec((tm, tn), lambda i, j: (i, j)),
            pl.BlockSpec((tm, LANE), lambda i, j: (i, 0)),
        ],
        out_shape=[
            jax.ShapeDtypeStruct((m, n), BF16),
            jax.ShapeDtypeStruct((m, LANE), BF16),
        ],
        scratch_shapes=[pltpu.VMEM((tm, d), BF16)],
        compiler_params=pltpu.CompilerParams(
            dimension_semantics=("parallel", "arbitrary"),
            vmem_limit_bytes=VMEM_LIMIT),
        name="in_proj",
    )(x2, gain, w_a, w_b, w_lr)


def _gla_kernel(q_ref, k_ref, v_ref, go_ref, lr_ref, wgu_ref, bg_ref, gn_ref,
                o_ref, s_ref, *, rows):
    @pl.when(pl.program_id(1) == 0)
    def _():
        s_ref[...] = jnp.zeros_like(s_ref)

    c = GLA_CHUNK
    r_idx = lax.broadcasted_iota(jnp.int32, (c, c), 0)
    c_idx = lax.broadcasted_iota(jnp.int32, (c, c), 1)
    causal = r_idx >= c_idx
    tril = causal.astype(BF16)
    scale = GLA_DK ** -0.5
    gn = gn_ref[...]

    def chunk(ci, carry):
        rs = pl.ds(pl.multiple_of(ci * c, c), c)
        pre = _dot(lr_ref[rs, :], wgu_ref[...]) + bg_ref[...]
        la = (jnp.minimum(pre, 0.0) - jnp.log(1.0 + jnp.exp(-jnp.abs(pre)))) * (1.0 / GATE_TAU)
        la_hi, la_lo = _split_bf16(la)
        b = _dot(tril, la_hi) + _dot(tril, la_lo)
        b_last = b[c - 1:c, :]
        q = q_ref[rs, :].astype(F32)
        k = k_ref[rs, :].astype(F32)
        q_dec = (q * scale * jnp.exp(b)).astype(BF16)
        k_intra = (k * jnp.exp(-b)).astype(BF16)
        k_state = (k * jnp.exp(b_last - b)).astype(BF16)
        decay = jnp.exp(b_last)
        for h in range(GLA_HEADS):
            ks = slice(h * GLA_DK, (h + 1) * GLA_DK)
            vs = slice(h * GLA_DV, (h + 1) * GLA_DV)
            v_h = v_ref[rs, vs]
            scores = _dot_nt(q_dec[:, ks], k_intra[:, ks])
            scores = jnp.where(causal, scores, 0.0).astype(BF16)
            st = s_ref[h]
            o = _dot(scores, v_h) + _dot_nt(q_dec[:, ks], st.astype(BF16))
            s_ref[h] = decay[:, ks] * st + _dot_tn(v_h, k_state[:, ks])
            o = _rms(o, gn)
            go = go_ref[rs, vs].astype(F32)
            o_ref[rs, vs] = (o * (go * _sigmoid(go))).astype(o_ref.dtype)
        return carry

    lax.fori_loop(0, rows // c, chunk, 0)


def _gla(z, z_lr, w_gu, b_gate, gla_norm, *, batch, seq, rows):
    m = z.shape[0]
    nb = seq // rows
    kw, vw = _KEY_W, _VAL_W
    row = lambda b, r: b * nb + r
    return pl.pallas_call(
        functools.partial(_gla_kernel, rows=rows),
        grid=(batch, nb),
        in_specs=[
            pl.BlockSpec((rows, kw), lambda b, r: (row(b, r), 0)),
            pl.BlockSpec((rows, kw), lambda b, r: (row(b, r), 1)),
            pl.BlockSpec((rows, vw), lambda b, r: (row(b, r), 1)),
            pl.BlockSpec((rows, vw), lambda b, r: (row(b, r), 2)),
            pl.BlockSpec((rows, LANE), lambda b, r: (row(b, r), 0)),
            pl.BlockSpec((LANE, kw), lambda b, r: (0, 0)),
            pl.BlockSpec((1, kw), lambda b, r: (0, 0)),
            pl.BlockSpec((1, GLA_DV), lambda b, r: (0, 0)),
        ],
        out_specs=pl.BlockSpec((rows, vw), lambda b, r: (row(b, r), 0)),
        out_shape=jax.ShapeDtypeStruct((m, vw), BF16),
        scratch_shapes=[pltpu.VMEM((GLA_HEADS, GLA_DV, GLA_DK), F32)],
        compiler_params=pltpu.CompilerParams(
            dimension_semantics=("parallel", "arbitrary"),
            vmem_limit_bytes=VMEM_LIMIT),
        name="gla",
    )(z, z, z, z, z_lr, w_gu, b_gate, gla_norm)


SB_SKIP_LOG2 = 151.0
LOG2_E = 1.4426950408889634


def _neg_abs(x):
    bits = lax.bitcast_convert_type(x, jnp.int32) | jnp.int32(-2 ** 31)
    return lax.bitcast_convert_type(bits, F32)


def _sb_kernel(q_ref, k_ref, v_ref, o_ref, acc_ref, carry_ref, *, tq, tk, heads):
    q0 = pl.program_id(1) * tq
    to_log2 = (SB_DH ** -0.5) * LOG2_E
    j_idx = lax.broadcasted_iota(jnp.int32, (tk, tk), 0)
    s_idx = lax.broadcasted_iota(jnp.int32, (tk, tk), 1)
    suffix_ones = (j_idx >= s_idx).astype(BF16)
    suffix_ones2 = jnp.concatenate([suffix_ones, suffix_ones], axis=0)
    t_loc = lax.broadcasted_iota(jnp.int32, (tq, tk), 0)
    s_loc = lax.broadcasted_iota(jnp.int32, (tq, tk), 1)
    hs = lambda h: slice(h * SB_DH, (h + 1) * SB_DH)

    def block(k0, mask, first):
        y, cs, carries = {}, {}, {}

        def scores(h):
            y[h] = _dot_nt(q_ref[:, hs(h)], k_ref[pl.ds(k0, tk), hs(h)]) * to_log2

        def suffix_sums(h):
            sp = jnp.maximum(y[h], 0.0) + jnp.log2(1.0 + jnp.exp2(_neg_abs(y[h])))
            if mask is not None:
                sp = jnp.where(mask, sp, 0.0)
            hi, lo = _split_bf16(sp)
            cs[h] = _dot(jnp.concatenate([hi, lo], axis=1), suffix_ones2)

        def weighted_values(h):
            a = jnp.exp2(y.pop(h) - cs[h])
            if mask is not None:
                a = jnp.where(mask, a, 0.0)
            pv = _dot(a.astype(BF16), v_ref[pl.ds(k0, tk), hs(h)])
            row_sum = cs.pop(h)[:, 0:1]
            if first:
                acc_ref[:, hs(h)] = pv
                carries[h] = row_sum
            else:
                carry = carry_ref[h]
                acc_ref[:, hs(h)] += pv * jnp.exp2(-carry)
                carries[h] = carry + row_sum
            carry_ref[h] = carries[h]

        stages = (scores, suffix_sums, weighted_values)
        for step in range(heads + len(stages) - 1):
            for lag, stage in enumerate(stages):
                if 0 <= step - lag < heads:
                    stage(step - lag)
        return jnp.min(functools.reduce(jnp.minimum, [carries[h] for h in range(heads)]))

    n_diag = tq // tk
    cmin = None
    for d in range(n_diag):
        k0 = pl.multiple_of(q0 + (n_diag - 1 - d) * tk, tk)
        cmin = block(k0, (s_loc + k0) < (t_loc + q0), d == 0)

    def cond(state):
        kb, cmin = state
        return jnp.logical_and(kb >= 0, cmin < SB_SKIP_LOG2)

    def body(state):
        kb, _ = state
        return kb - 1, block(pl.multiple_of(kb * tk, tk), None, False)

    lax.while_loop(cond, body, (q0 // tk - 1, cmin))
    o_ref[...] = acc_ref[...].astype(o_ref.dtype)


def _sb_attn(z, *, batch, seq, heads, q_off, k_off, v_off, tq, tk):
    m = z.shape[0]
    nq = seq // tq
    w = heads * SB_DH
    return pl.pallas_call(
        functools.partial(_sb_kernel, tq=tq, tk=tk, heads=heads),
        grid=(batch, nq),
        in_specs=[
            pl.BlockSpec((tq, w), lambda b, i: (b * nq + i, q_off)),
            pl.BlockSpec((seq, w), lambda b, i: (b, k_off)),
            pl.BlockSpec((seq, w), lambda b, i: (b, v_off)),
        ],
        out_specs=pl.BlockSpec((tq, w), lambda b, i: (b * nq + i, 0)),
        out_shape=jax.ShapeDtypeStruct((m, w), BF16),
        scratch_shapes=[pltpu.VMEM((tq, w), F32), pltpu.VMEM((heads, tq, 1), F32)],
        compiler_params=pltpu.CompilerParams(
            dimension_semantics=("parallel", "arbitrary"),
            vmem_limit_bytes=VMEM_LIMIT),
        name="sb_attn",
    )(z, z, z)


def _mix_out_kernel(oa_ref, ob_ref, ga_ref, gb_ref, x_ref, wa_ref, wb_ref, wo_ref, g_ref, h_ref):
    ya = _dot(oa_ref[...], wa_ref[...])
    yb = _dot(ob_ref[...], wb_ref[...])
    y = _sigmoid(ga_ref[...].astype(F32)) * ya + _sigmoid(gb_ref[...].astype(F32)) * yb
    mix = _dot(y.astype(BF16), wo_ref[...])
    h_ref[...] = x_ref[...] + _rms(mix, g_ref[...])


def _resident(shape):
    return pl.BlockSpec(shape, lambda *_: (0,) * len(shape), pipeline_mode=pl.Buffered(1))


def _mix_out(o_a, o_b, z, x2, w_a, w_b, w_o, gain, *, ga_blk, gb_blk, tm):
    m, d = x2.shape
    return pl.pallas_call(
        _mix_out_kernel,
        grid=(m // tm,),
        in_specs=[
            pl.BlockSpec((tm, o_a.shape[1]), lambda i: (i, 0)),
            pl.BlockSpec((tm, o_b.shape[1]), lambda i: (i, 0)),
            pl.BlockSpec((tm, d), lambda i: (i, ga_blk)),
            pl.BlockSpec((tm, d), lambda i: (i, gb_blk)),
            pl.BlockSpec((tm, d), lambda i: (i, 0)),
            _resident(w_a.shape),
            _resident(w_b.shape),
            _resident(w_o.shape),
            _resident(gain.shape),
        ],
        out_specs=pl.BlockSpec((tm, d), lambda i: (i, 0)),
        out_shape=jax.ShapeDtypeStruct((m, d), F32),
        compiler_params=pltpu.CompilerParams(
            dimension_semantics=("parallel",),
            vmem_limit_bytes=VMEM_LIMIT),
        name="mix_out",
    )(o_a, o_b, z, z, x2, w_a, w_b, w_o, gain)


def _mlp_kernel(h_ref, gpre_ref, wu_ref, wd_ref, gpost_ref, o_ref, u_ref, acc_ref):
    f = pl.program_id(1)

    @pl.when(f == 0)
    def _():
        u_ref[...] = _rms(h_ref[...], gpre_ref[...]).astype(BF16)
        acc_ref[...] = jnp.zeros_like(acc_ref)

    a = jnp.maximum(_dot(u_ref[...], wu_ref[...]), 0.0)
    acc_ref[...] += _dot((a * a).astype(BF16), wd_ref[...])

    @pl.when(f == pl.num_programs(1) - 1)
    def _():
        o_ref[...] = h_ref[...] + _rms(acc_ref[...], gpost_ref[...])


def _mlp(h, g_pre, w_up, w_down, g_post, *, tm, tf):
    m, d = h.shape
    ff = w_up.shape[1]
    return pl.pallas_call(
        _mlp_kernel,
        grid=(m // tm, ff // tf),
        in_specs=[
            pl.BlockSpec((tm, d), lambda i, f: (i, 0)),
            pl.BlockSpec((1, d), lambda i, f: (0, 0)),
            pl.BlockSpec((d, tf), lambda i, f: (0, f)),
            pl.BlockSpec((tf, d), lambda i, f: (f, 0)),
            pl.BlockSpec((1, d), lambda i, f: (0, 0)),
        ],
        out_specs=pl.BlockSpec((tm, d), lambda i, f: (i, 0)),
        out_shape=jax.ShapeDtypeStruct((m, d), F32),
        scratch_shapes=[pltpu.VMEM((tm, d), BF16), pltpu.VMEM((tm, d), F32)],
        compiler_params=pltpu.CompilerParams(
            dimension_semantics=("parallel", "arbitrary"),
            vmem_limit_bytes=VMEM_LIMIT),
        name="mlp",
    )(h, g_pre, w_up, w_down, g_post)


def _ple_kernel(h_ref, p_ref, g_ref, wg_ref, wp_ref, o_ref):
    h = h_ref[...]
    e = _dot(p_ref[...].astype(BF16), wp_ref[...])
    gate = _sigmoid(_dot(_rms(h, g_ref[...]).astype(BF16), wg_ref[...]))
    o_ref[...] = h + gate * e


def _ple(h, p2, gain, w_gate, w_proj, *, tm):
    m, d = h.shape
    return pl.pallas_call(
        _ple_kernel,
        grid=(m // tm,),
        in_specs=[
            pl.BlockSpec((tm, d), lambda i: (i, 0)),
            pl.BlockSpec((tm, p2.shape[1]), lambda i: (i, 0)),
            _resident(gain.shape),
            _resident(w_gate.shape),
            _resident(w_proj.shape),
        ],
        out_specs=pl.BlockSpec((tm, d), lambda i: (i, 0)),
        out_shape=jax.ShapeDtypeStruct((m, d), F32),
        compiler_params=pltpu.CompilerParams(
            dimension_semantics=("parallel",),
            vmem_limit_bytes=VMEM_LIMIT),
        name="ple",
    )(h, p2, gain, w_gate, w_proj)


def _split_w_in(w_in):
    lr0 = 2 * _KEY_W + _VAL_W
    w_a = w_in[:, :lr0].astype(BF16)
    w_b = w_in[:, lr0 + GATE_RANK:].astype(BF16)
    w_lr = jnp.pad(w_in[:, lr0:lr0 + GATE_RANK], ((0, 0), (0, LANE - GATE_RANK))).astype(BF16)
    return w_a, w_b, w_lr


def kernel(x, p, norm_mix_pre, norm_mix_post, w_in, w_gate_up, b_gate, gla_norm, w_branch_gla,
           w_branch_sb, w_out, norm_mlp_pre, norm_mlp_post, w_mlp_up, w_mlp_down, norm_ple,
           w_ple_gate, w_ple_proj):
    batch, seq, d = x.shape
    depth = w_in.shape[0]
    m = batch * seq
    sb_w = d // 2
    sb_heads = sb_w // SB_DH
    h = x.reshape(m, d)
    row = lambda g: g.reshape(1, -1)
    for i in range(depth):
        w_a, w_b, w_lr = _split_w_in(w_in[i])
        w_gu = jnp.pad(w_gate_up[i], ((0, LANE - GATE_RANK), (0, 0))).astype(BF16)
        z, z_lr = _in_proj(h, row(norm_mix_pre[i]), w_a, w_b, w_lr, tm=1024, tn=1024)
        o_a = _gla(z, z_lr, w_gu, row(b_gate[i]), row(gla_norm[i]), batch=batch, seq=seq, rows=512)
        sq_off = (2 * _KEY_W + 2 * _VAL_W) // sb_w
        o_b = _sb_attn(z, batch=batch, seq=seq, heads=sb_heads, q_off=sq_off,
                       k_off=sq_off + 1, v_off=sq_off + 2, tq=256, tk=256)
        gate_blk = (2 * _KEY_W + 2 * _VAL_W + 3 * sb_w) // d
        h = _mix_out(o_a, o_b, z, h, w_branch_gla[i].astype(BF16), w_branch_sb[i].astype(BF16),
                     w_out[i].astype(BF16), row(norm_mix_post[i]),
                     ga_blk=gate_blk, gb_blk=gate_blk + 1, tm=512)
        h = _mlp(h, row(norm_mlp_pre[i]), w_mlp_up[i].astype(BF16), w_mlp_down[i].astype(BF16),
                 row(norm_mlp_post[i]), tm=512, tf=1024)
        h = _ple(h, p[i].reshape(m, -1), row(norm_ple[i]), w_ple_gate[i].astype(BF16),
                 w_ple_proj[i].astype(BF16), tm=512)
    return h.reshape(batch, seq, d)
```

```python
import functools

import jax
import jax.numpy as jnp
from jax import lax
from jax.experimental import pallas as pl
from jax.experimental.pallas import tpu as pltpu

F32 = jnp.float32
BF16 = jnp.bfloat16

EPS = 1e-6
GLA_HEADS = 4
GLA_DK = 128
GLA_DV = 256
GATE_RANK = 16
GATE_TAU = 16.0
GLA_CHUNK = 64
SB_DH = 128
LANE = 128

_KEY_W = GLA_HEADS * GLA_DK
_VAL_W = GLA_HEADS * GLA_DV

VMEM_LIMIT = 56 * 1024 * 1024


def _rms(x, gain):
    ms = jnp.mean(x * x, axis=-1, keepdims=True)
    return x * lax.rsqrt(ms + EPS) * gain


def _sigmoid(x):
    return 1.0 / (1.0 + jnp.exp(-x))


def _split_bf16(x):
    hi = x.astype(BF16)
    lo = (x - hi.astype(F32)).astype(BF16)
    return hi, lo


def _dot(a, b):
    return jnp.dot(a, b, preferred_element_type=F32)


def _dot_nt(a, b):
    return lax.dot_general(a, b, (((1,), (1,)), ((), ())), preferred_element_type=F32)


def _dot_tn(a, b):
    return lax.dot_general(a, b, (((0,), (0,)), ((), ())), preferred_element_type=F32)


def _in_proj_kernel(x_ref, g_ref, wa_ref, wb_ref, wlr_ref, z_ref, zlr_ref, u_ref, *, na):
    j = pl.program_id(1)

    @pl.when(j == 0)
    def _():
        u = _rms(x_ref[...], g_ref[...]).astype(BF16)
        u_ref[...] = u
        zlr_ref[...] = _dot(u, wlr_ref[...]).astype(zlr_ref.dtype)

    @pl.when(j < na)
    def _():
        z_ref[...] = _dot(u_ref[...], wa_ref[...]).astype(z_ref.dtype)

    @pl.when(j >= na)
    def _():
        z_ref[...] = _dot(u_ref[...], wb_ref[...]).astype(z_ref.dtype)


def _in_proj(x2, gain, w_a, w_b, w_lr, *, tm, tn):
    m, d = x2.shape
    na, nb = w_a.shape[1] // tn, w_b.shape[1] // tn
    n = w_a.shape[1] + w_b.shape[1]
    return pl.pallas_call(
        functools.partial(_in_proj_kernel, na=na),
        grid=(m // tm, n // tn),
        in_specs=[
            pl.BlockSpec((tm, d), lambda i, j: (i, 0)),
            pl.BlockSpec((1, d), lambda i, j: (0, 0)),
            pl.BlockSpec((d, tn), lambda i, j: (0, jnp.where(j < na, j, 0))),
            pl.BlockSpec((d, tn), lambda i, j: (0, jnp.where(j < na, nb - 1, j - na))),
            pl.BlockSpec((d, LANE), lambda i, j: (0, 0)),
        ],
        out_specs=[
            pl.BlockSpec((tm, tn), lambda i, j: (i, j)),
            pl.BlockSpec((tm, LANE), lambda i, j: (i, 0)),
        ],
        out_shape=[
            jax.ShapeDtypeStruct((m, n), BF16),
            jax.ShapeDtypeStruct((m, LANE), BF16),
        ],
        scratch_shapes=[pltpu.VMEM((tm, d), BF16)],
        compiler_params=pltpu.CompilerParams(
            dimension_semantics=("parallel", "arbitrary"),
            vmem_limit_bytes=VMEM_LIMIT),
        name="in_proj",
    )(x2, gain, w_a, w_b, w_lr)


def _gla_kernel(q_ref, k_ref, v_ref, go_ref, lr_ref, wgu_ref, bg_ref, gn_ref,
                o_ref, s_ref, *, rows):
    @pl.when(pl.program_id(1) == 0)
    def _():
        s_ref[...] = jnp.zeros_like(s_ref)

    c = GLA_CHUNK
    nc = rows // c
    r_idx = lax.broadcasted_iota(jnp.int32, (c, c), 0)
    c_idx = lax.broadcasted_iota(jnp.int32, (c, c), 1)
    causal = r_idx >= c_idx
    tril = causal.astype(BF16)
    tril2 = jnp.concatenate([tril, tril], axis=1)
    scale = GLA_DK ** -0.5
    gn = gn_ref[...]
    ksl = lambda h: slice(h * GLA_DK, (h + 1) * GLA_DK)
    vsl = lambda h: slice(h * GLA_DV, (h + 1) * GLA_DV)

    pre = _dot(lr_ref[...], wgu_ref[...]) + bg_ref[...]
    la = (jnp.minimum(pre, 0.0) - jnp.log(1.0 + jnp.exp(-jnp.abs(pre)))) * (1.0 / GATE_TAU)
    la_hi, la_lo = _split_bf16(la)

    q_dec, k_intra, k_state, decay = [], [], [], []
    for ci in range(nc):
        rs = slice(ci * c, (ci + 1) * c)
        b = _dot(tril2, jnp.concatenate([la_hi[rs], la_lo[rs]], axis=0))
        b_last = b[c - 1:c, :]
        q = q_ref[rs, :].astype(F32)
        k = k_ref[rs, :].astype(F32)
        q_dec.append((q * scale * jnp.exp(b)).astype(BF16))
        k_intra.append((k * jnp.exp(-b)).astype(BF16))
        k_state.append((k * jnp.exp(b_last - b)).astype(BF16))
        decay.append(jnp.exp(b_last))

    def intra(ci):
        rs = slice(ci * c, (ci + 1) * c)
        out = []
        for h in range(GLA_HEADS):
            v_h = v_ref[rs, vsl(h)]
            scores = _dot_nt(q_dec[ci][:, ksl(h)], k_intra[ci][:, ksl(h)])
            scores = jnp.where(causal, scores, 0.0).astype(BF16)
            out.append((_dot(scores, v_h), _dot_tn(v_h, k_state[ci][:, ksl(h)])))
        return out

    def inter(ci, intra_ci):
        rs = slice(ci * c, (ci + 1) * c)
        for h in range(GLA_HEADS):
            o_intra, kv = intra_ci[h]
            st = s_ref[h]
            o = o_intra + _dot_nt(q_dec[ci][:, ksl(h)], st.astype(BF16))
            s_ref[h] = decay[ci][:, ksl(h)] * st + kv
            o = _rms(o, gn)
            go = go_ref[rs, vsl(h)].astype(F32)
            o_ref[rs, vsl(h)] = (o * (go * _sigmoid(go))).astype(o_ref.dtype)

    ahead = intra(0)
    for ci in range(nc):
        current, ahead = ahead, (intra(ci + 1) if ci + 1 < nc else None)
        inter(ci, current)


def _gla(z, z_lr, w_gu, b_gate, gla_norm, *, batch, seq, rows):
    m = z.shape[0]
    nb = seq // rows
    kw, vw = _KEY_W, _VAL_W
    row = lambda b, r: b * nb + r
    return pl.pallas_call(
        functools.partial(_gla_kernel, rows=rows),
        grid=(batch, nb),
        in_specs=[
            pl.BlockSpec((rows, kw), lambda b, r: (row(b, r), 0)),
            pl.BlockSpec((rows, kw), lambda b, r: (row(b, r), 1)),
            pl.BlockSpec((rows, vw), lambda b, r: (row(b, r), 1)),
            pl.BlockSpec((rows, vw), lambda b, r: (row(b, r), 2)),
            pl.BlockSpec((rows, LANE), lambda b, r: (row(b, r), 0)),
            pl.BlockSpec((LANE, kw), lambda b, r: (0, 0)),
            pl.BlockSpec((1, kw), lambda b, r: (0, 0)),
            pl.BlockSpec((1, GLA_DV), lambda b, r: (0, 0)),
        ],
        out_specs=pl.BlockSpec((rows, vw), lambda b, r: (row(b, r), 0)),
        out_shape=jax.ShapeDtypeStruct((m, vw), BF16),
        scratch_shapes=[pltpu.VMEM((GLA_HEADS, GLA_DV, GLA_DK), F32)],
        compiler_params=pltpu.CompilerParams(
            dimension_semantics=("parallel", "arbitrary"),
            vmem_limit_bytes=VMEM_LIMIT),
        name="gla",
    )(z, z, z, z, z_lr, w_gu, b_gate, gla_norm)


SB_SKIP_LOG2 = 151.0
LOG2_E = 1.4426950408889634


def _sb_kernel(q_ref, k_ref, v_ref, o_ref, acc_ref, carry_ref, *, tq, tk, heads):
    q0 = pl.program_id(1) * tq
    to_log2 = (SB_DH ** -0.5) * LOG2_E
    j_idx = lax.broadcasted_iota(jnp.int32, (tk, tk), 0)
    s_idx = lax.broadcasted_iota(jnp.int32, (tk, tk), 1)
    suffix_ones = (j_idx >= s_idx).astype(BF16)
    suffix_ones2 = jnp.concatenate([suffix_ones, suffix_ones], axis=0)
    t_loc = lax.broadcasted_iota(jnp.int32, (tq, tk), 0)
    s_loc = lax.broadcasted_iota(jnp.int32, (tq, tk), 1)
    hs = lambda h: slice(h * SB_DH, (h + 1) * SB_DH)

    def block(k0, mask, first):
        y, cs, carries = {}, {}, {}

        def scores(h):
            y[h] = _dot_nt(q_ref[:, hs(h)], k_ref[pl.ds(k0, tk), hs(h)]) * to_log2

        def suffix_sums(h):
            sp = jnp.maximum(y[h], 0.0) + jnp.log2(1.0 + jnp.exp2(-jnp.abs(y[h])))
            if mask is not None:
                sp = jnp.where(mask, sp, 0.0)
            hi, lo = _split_bf16(sp)
            cs[h] = _dot(jnp.concatenate([hi, lo], axis=1), suffix_ones2)

        def weighted_values(h):
            a = jnp.exp2(y.pop(h) - cs[h])
            if mask is not None:
                a = jnp.where(mask, a, 0.0)
            pv = _dot(a.astype(BF16), v_ref[pl.ds(k0, tk), hs(h)])
            row_sum = cs.pop(h)[:, 0:1]
            if first:
                acc_ref[:, hs(h)] = pv
                carries[h] = row_sum
            else:
                carry = carry_ref[h]
                acc_ref[:, hs(h)] += pv * jnp.exp2(-carry)
                carries[h] = carry + row_sum
            carry_ref[h] = carries[h]

        stages = (scores, suffix_sums, weighted_values)
        for step in range(heads + len(stages) - 1):
            for lag, stage in enumerate(stages):
                if 0 <= step - lag < heads:
                    stage(step - lag)
        return jnp.min(functools.reduce(jnp.minimum, [carries[h] for h in range(heads)]))

    n_diag = tq // tk
    cmin = None
    for d in range(n_diag):
        k0 = pl.multiple_of(q0 + (n_diag - 1 - d) * tk, tk)
        cmin = block(k0, (s_loc + k0) < (t_loc + q0), d == 0)

    def cond(state):
        kb, cmin = state
        return jnp.logical_and(kb >= 0, cmin < SB_SKIP_LOG2)

    def body(state):
        kb, _ = state
        return kb - 1, block(pl.multiple_of(kb * tk, tk), None, False)

    lax.while_loop(cond, body, (q0 // tk - 1, cmin))
    o_ref[...] = acc_ref[...].astype(o_ref.dtype)


def _sb_attn(z, *, batch, seq, heads, q_off, k_off, v_off, tq, tk):
    m = z.shape[0]
    nq = seq // tq
    w = heads * SB_DH
    return pl.pallas_call(
        functools.partial(_sb_kernel, tq=tq, tk=tk, heads=heads),
        grid=(batch, nq),
        in_specs=[
            pl.BlockSpec((tq, w), lambda b, i: (b * nq + i, q_off)),
            pl.BlockSpec((seq, w), lambda b, i: (b, k_off)),
            pl.BlockSpec((seq, w), lambda b, i: (b, v_off)),
        ],
        out_specs=pl.BlockSpec((tq, w), lambda b, i: (b * nq + i, 0)),
        out_shape=jax.ShapeDtypeStruct((m, w), BF16),
        scratch_shapes=[pltpu.VMEM((tq, w), F32), pltpu.VMEM((heads, tq, 1), F32)],
        compiler_params=pltpu.CompilerParams(
            dimension_semantics=("parallel", "arbitrary"),
            vmem_limit_bytes=VMEM_LIMIT),
        name="sb_attn",
    )(z, z, z)


def _mix_out_kernel(oa_ref, ob_ref, ga_ref, gb_ref, x_ref, wa_ref, wb_ref, wo_ref, g_ref, h_ref):
    ya = _dot(oa_ref[...], wa_ref[...])
    yb = _dot(ob_ref[...], wb_ref[...])
    y = _sigmoid(ga_ref[...].astype(F32)) * ya + _sigmoid(gb_ref[...].astype(F32)) * yb
    mix = _dot(y.astype(BF16), wo_ref[...])
    h_ref[...] = x_ref[...] + _rms(mix, g_ref[...])


def _resident(shape):
    return pl.BlockSpec(shape, lambda *_: (0,) * len(shape), pipeline_mode=pl.Buffered(1))


def _mix_out(o_a, o_b, z, x2, w_a, w_b, w_o, gain, *, ga_blk, gb_blk, tm):
    m, d = x2.shape
    return pl.pallas_call(
        _mix_out_kernel,
        grid=(m // tm,),
        in_specs=[
            pl.BlockSpec((tm, o_a.shape[1]), lambda i: (i, 0)),
            pl.BlockSpec((tm, o_b.shape[1]), lambda i: (i, 0)),
            pl.BlockSpec((tm, d), lambda i: (i, ga_blk)),
            pl.BlockSpec((tm, d), lambda i: (i, gb_blk)),
            pl.BlockSpec((tm, d), lambda i: (i, 0)),
            _resident(w_a.shape),
            _resident(w_b.shape),
            _resident(w_o.shape),
            _resident(gain.shape),
        ],
        out_specs=pl.BlockSpec((tm, d), lambda i: (i, 0)),
        out_shape=jax.ShapeDtypeStruct((m, d), F32),
        compiler_params=pltpu.CompilerParams(
            dimension_semantics=("parallel",),
            vmem_limit_bytes=VMEM_LIMIT),
        name="mix_out",
    )(o_a, o_b, z, z, x2, w_a, w_b, w_o, gain)


def _mlp_kernel(h_ref, gpre_ref, wu_ref, wd_ref, gpost_ref, o_ref, u_ref, acc_ref):
    f = pl.program_id(1)

    @pl.when(f == 0)
    def _():
        u_ref[...] = _rms(h_ref[...], gpre_ref[...]).astype(BF16)
        acc_ref[...] = jnp.zeros_like(acc_ref)

    a = jnp.maximum(_dot(u_ref[...], wu_ref[...]), 0.0)
    acc_ref[...] += _dot((a * a).astype(BF16), wd_ref[...])

    @pl.when(f == pl.num_programs(1) - 1)
    def _():
        o_ref[...] = h_ref[...] + _rms(acc_ref[...], gpost_ref[...])


def _mlp(h, g_pre, w_up, w_down, g_post, *, tm, tf):
    m, d = h.shape
    ff = w_up.shape[1]
    return pl.pallas_call(
        _mlp_kernel,
        grid=(m // tm, ff // tf),
        in_specs=[
            pl.BlockSpec((tm, d), lambda i, f: (i, 0)),
            pl.BlockSpec((1, d), lambda i, f: (0, 0)),
            pl.BlockSpec((d, tf), lambda i, f: (0, f)),
            pl.BlockSpec((tf, d), lambda i, f: (f, 0)),
            pl.BlockSpec((1, d), lambda i, f: (0, 0)),
        ],
        out_specs=pl.BlockSpec((tm, d), lambda i, f: (i, 0)),
        out_shape=jax.ShapeDtypeStruct((m, d), F32),
        scratch_shapes=[pltpu.VMEM((tm, d), BF16), pltpu.VMEM((tm, d), F32)],
        compiler_params=pltpu.CompilerParams(
            dimension_semantics=("parallel", "arbitrary"),
            vmem_limit_bytes=VMEM_LIMIT),
        name="mlp",
    )(h, g_pre, w_up, w_down, g_post)


def _ple_kernel(h_ref, p_ref, g_ref, wg_ref, wp_ref, o_ref):
    h = h_ref[...]
    e = _dot(p_ref[...].astype(BF16), wp_ref[...])
    gate = _sigmoid(_dot(_rms(h, g_ref[...]).astype(BF16), wg_ref[...]))
    o_ref[...] = h + gate * e


def _ple(h, p2, gain, w_gate, w_proj, *, tm):
    m, d = h.shape
    return pl.pallas_call(
        _ple_kernel,
        grid=(m // tm,),
        in_specs=[
            pl.BlockSpec((tm, d), lambda i: (i, 0)),
            pl.BlockSpec((tm, p2.shape[1]), lambda i: (i, 0)),
            _resident(gain.shape),
            _resident(w_gate.shape),
            _resident(w_proj.shape),
        ],
        out_specs=pl.BlockSpec((tm, d), lambda i: (i, 0)),
        out_shape=jax.ShapeDtypeStruct((m, d), F32),
        compiler_params=pltpu.CompilerParams(
            dimension_semantics=("parallel",),
            vmem_limit_bytes=VMEM_LIMIT),
        name="ple",
    )(h, p2, gain, w_gate, w_proj)


def _split_w_in(w_in):
    lr0 = 2 * _KEY_W + _VAL_W
    w_a = w_in[:, :lr0].astype(BF16)
    w_b = w_in[:, lr0 + GATE_RANK:].astype(BF16)
    w_lr = jnp.pad(w_in[:, lr0:lr0 + GATE_RANK], ((0, 0), (0, LANE - GATE_RANK))).astype(BF16)
    return w_a, w_b, w_lr


def kernel(x, p, norm_mix_pre, norm_mix_post, w_in, w_gate_up, b_gate, gla_norm, w_branch_gla,
           w_branch_sb, w_out, norm_mlp_pre, norm_mlp_post, w_mlp_up, w_mlp_down, norm_ple,
           w_ple_gate, w_ple_proj):
    batch, seq, d = x.shape
    depth = w_in.shape[0]
    m = batch * seq
    sb_w = d // 2
    sb_heads = sb_w // SB_DH
    h = x.reshape(m, d)
    row = lambda g: g.reshape(1, -1)
    for i in range(depth):
        w_a, w_b, w_lr = _split_w_in(w_in[i])
        w_gu = jnp.pad(w_gate_up[i], ((0, LANE - GATE_RANK), (0, 0))).astype(BF16)
        z, z_lr = _in_proj(h, row(norm_mix_pre[i]), w_a, w_b, w_lr, tm=1024, tn=1024)
        o_a = _gla(z, z_lr, w_gu, row(b_gate[i]), row(gla_norm[i]), batch=batch, seq=seq, rows=512)
        sq_off = (2 * _KEY_W + 2 * _VAL_W) // sb_w
        o_b = _sb_attn(z, batch=batch, seq=seq, heads=sb_heads, q_off=sq_off,
                       k_off=sq_off + 1, v_off=sq_off + 2, tq=256, tk=256)
        gate_blk = (2 * _KEY_W + 2 * _VAL_W + 3 * sb_w) // d
        h = _mix_out(o_a, o_b, z, h, w_branch_gla[i].astype(BF16), w_branch_sb[i].astype(BF16),
                     w_out[i].astype(BF16), row(norm_mix_post[i]),
                     ga_blk=gate_blk, gb_blk=gate_blk + 1, tm=512)
        h = _mlp(h, row(norm_mlp_pre[i]), w_mlp_up[i].astype(BF16), w_mlp_down[i].astype(BF16),
                 row(norm_mlp_post[i]), tm=512, tf=1024)
        h = _ple(h, p[i].reshape(m, -1), row(norm_ple[i]), w_ple_gate[i].astype(BF16),
                 w_ple_proj[i].astype(BF16), tm=512)
    return h.reshape(batch, seq, d)
```

```python
import functools

import jax
import jax.numpy as jnp
from jax import lax
from jax.experimental import pallas as pl
from jax.experimental.pallas import tpu as pltpu

F32 = jnp.float32
BF16 = jnp.bfloat16

EPS = 1e-6
GLA_HEADS = 4
GLA_DK = 128
GLA_DV = 256
GATE_RANK = 16
GATE_TAU = 16.0
GLA_CHUNK = 64
SB_DH = 128
LANE = 128
BF16_SUBLANES = 16

_KEY_W = GLA_HEADS * GLA_DK
_VAL_W = GLA_HEADS * GLA_DV

VMEM_LIMIT = 56 * 1024 * 1024


def _rms(x, gain):
    ms = jnp.mean(x * x, axis=-1, keepdims=True)
    return x * lax.rsqrt(ms + EPS) * gain


def _sigmoid(x):
    return 1.0 / (1.0 + jnp.exp(-x))


def _split_bf16(x):
    hi = x.astype(BF16)
    lo = (x - hi.astype(F32)).astype(BF16)
    return hi, lo


def _dot(a, b):
    return jnp.dot(a, b, preferred_element_type=F32)


def _dot_nt(a, b):
    return lax.dot_general(a, b, (((1,), (1,)), ((), ())), preferred_element_type=F32)


def _dot_tn(a, b):
    return lax.dot_general(a, b, (((0,), (0,)), ((), ())), preferred_element_type=F32)


def _in_proj_kernel(x_ref, g_ref, wa_ref, wb_ref, wlr_ref, z_ref, zlr_ref, u_ref, *, na):
    j = pl.program_id(1)

    @pl.when(j == 0)
    def _():
        u = _rms(x_ref[...], g_ref[...]).astype(BF16)
        u_ref[...] = u
        zlr_ref[...] = _dot(u, wlr_ref[...]).astype(zlr_ref.dtype)

    @pl.when(j < na)
    def _():
        z_ref[...] = _dot(u_ref[...], wa_ref[...]).astype(z_ref.dtype)

    @pl.when(j >= na)
    def _():
        z_ref[...] = _dot(u_ref[...], wb_ref[...]).astype(z_ref.dtype)


def _in_proj(x2, gain, w_a, w_b, w_lr, *, tm, tn):
    m, d = x2.shape
    na, nb = w_a.shape[1] // tn, w_b.shape[1] // tn
    n = w_a.shape[1] + w_b.shape[1]
    return pl.pallas_call(
        functools.partial(_in_proj_kernel, na=na),
        grid=(m // tm, n // tn),
        in_specs=[
            pl.BlockSpec((tm, d), lambda i, j: (i, 0)),
            pl.BlockSpec((1, d), lambda i, j: (0, 0)),
            pl.BlockSpec((d, tn), lambda i, j: (0, jnp.where(j < na, j, 0))),
            pl.BlockSpec((d, tn), lambda i, j: (0, jnp.where(j < na, nb - 1, j - na))),
            pl.BlockSpec((d, LANE), lambda i, j: (0, 0)),
        ],
        out_specs=[
            pl.BlockSpec((tm, tn), lambda i, j: (i, j)),
            pl.BlockSpec((tm, LANE), lambda i, j: (i, 0)),
        ],
        out_shape=[
            jax.ShapeDtypeStruct((m, n), BF16),
            jax.ShapeDtypeStruct((m, LANE), BF16),
        ],
        scratch_shapes=[pltpu.VMEM((tm, d), BF16)],
        compiler_params=pltpu.CompilerParams(
            dimension_semantics=("parallel", "arbitrary"),
            vmem_limit_bytes=VMEM_LIMIT),
        name="in_proj",
    )(x2, gain, w_a, w_b, w_lr)


def _gla_kernel(q_ref, k_ref, v_ref, go_ref, lr_ref, wgu_ref, bg_ref, gn_ref,
                o_ref, s_ref, *, rows):
    @pl.when(pl.program_id(1) == 0)
    def _():
        s_ref[...] = jnp.zeros_like(s_ref)

    c = GLA_CHUNK
    nc = rows // c
    r_idx = lax.broadcasted_iota(jnp.int32, (c, c), 0)
    c_idx = lax.broadcasted_iota(jnp.int32, (c, c), 1)
    causal = r_idx >= c_idx
    tril = causal.astype(BF16)
    tril2 = jnp.concatenate([tril, tril], axis=1)
    scale = GLA_DK ** -0.5
    gn = gn_ref[...]
    ksl = lambda h: slice(h * GLA_DK, (h + 1) * GLA_DK)
    vsl = lambda h: slice(h * GLA_DV, (h + 1) * GLA_DV)

    pre = _dot(lr_ref[...], wgu_ref[...]) + bg_ref[...]
    la = (jnp.minimum(pre, 0.0) - jnp.log(1.0 + jnp.exp(-jnp.abs(pre)))) * (1.0 / GATE_TAU)
    la_hi, la_lo = _split_bf16(la)

    q_dec, k_intra, k_state, decay = [], [], [], []
    for ci in range(nc):
        rs = slice(ci * c, (ci + 1) * c)
        b = _dot(tril2, jnp.concatenate([la_hi[rs], la_lo[rs]], axis=0))
        b_last = b[c - 1:c, :]
        q = q_ref[rs, :].astype(F32)
        k = k_ref[rs, :].astype(F32)
        q_dec.append((q * scale * jnp.exp(b)).astype(BF16))
        k_intra.append((k * jnp.exp(-b)).astype(BF16))
        k_state.append((k * jnp.exp(b_last - b)).astype(BF16))
        decay.append(jnp.exp(b_last))

    def intra(ci):
        rs = slice(ci * c, (ci + 1) * c)
        out = []
        for h in range(GLA_HEADS):
            v_h = v_ref[rs, vsl(h)]
            scores = _dot_nt(q_dec[ci][:, ksl(h)], k_intra[ci][:, ksl(h)])
            scores = jnp.where(causal, scores, 0.0).astype(BF16)
            out.append((_dot(scores, v_h), _dot_tn(v_h, k_state[ci][:, ksl(h)])))
        return out

    def inter(ci, intra_ci):
        rs = slice(ci * c, (ci + 1) * c)
        for h in range(GLA_HEADS):
            o_intra, kv = intra_ci[h]
            st = s_ref[h]
            o = o_intra + _dot_nt(q_dec[ci][:, ksl(h)], st.astype(BF16))
            s_ref[h] = decay[ci][:, ksl(h)] * st + kv
            o = _rms(o, gn)
            go = go_ref[rs, vsl(h)].astype(F32)
            o_ref[rs, vsl(h)] = (o * (go * _sigmoid(go))).astype(o_ref.dtype)

    ahead = intra(0)
    for ci in range(nc):
        current, ahead = ahead, (intra(ci + 1) if ci + 1 < nc else None)
        inter(ci, current)


def _gla(z, z_lr, w_gu, b_gate, gla_norm, *, batch, seq, rows):
    m = z.shape[0]
    nb = seq // rows
    kw, vw = _KEY_W, _VAL_W
    row = lambda b, r: b * nb + r
    return pl.pallas_call(
        functools.partial(_gla_kernel, rows=rows),
        grid=(batch, nb),
        in_specs=[
            pl.BlockSpec((rows, kw), lambda b, r: (row(b, r), 0)),
            pl.BlockSpec((rows, kw), lambda b, r: (row(b, r), 1)),
            pl.BlockSpec((rows, vw), lambda b, r: (row(b, r), 1)),
            pl.BlockSpec((rows, vw), lambda b, r: (row(b, r), 2)),
            pl.BlockSpec((rows, LANE), lambda b, r: (row(b, r), 0)),
            pl.BlockSpec((LANE, kw), lambda b, r: (0, 0)),
            pl.BlockSpec((1, kw), lambda b, r: (0, 0)),
            pl.BlockSpec((1, GLA_DV), lambda b, r: (0, 0)),
        ],
        out_specs=pl.BlockSpec((rows, vw), lambda b, r: (row(b, r), 0)),
        out_shape=jax.ShapeDtypeStruct((m, vw), BF16),
        scratch_shapes=[pltpu.VMEM((GLA_HEADS, GLA_DV, GLA_DK), F32)],
        compiler_params=pltpu.CompilerParams(
            dimension_semantics=("parallel", "arbitrary"),
            vmem_limit_bytes=VMEM_LIMIT),
        name="gla",
    )(z, z, z, z, z_lr, w_gu, b_gate, gla_norm)


SB_SKIP_LOG2 = 151.0
LOG2_E = 1.4426950408889634


def _sb_kernel(q_ref, k_ref, v_ref, *rest, tq, tk, heads, n_cast):
    w32_refs, o_ref, w16_refs = rest[:n_cast], rest[n_cast], rest[n_cast + 1:2 * n_cast + 1]
    acc_ref, carry_ref = rest[2 * n_cast + 1:]
    for w32, w16 in zip(w32_refs, w16_refs):
        w16[...] = w32[...].astype(w16.dtype)

    q0 = pl.program_id(1) * tq
    to_log2 = (SB_DH ** -0.5) * LOG2_E
    j_idx = lax.broadcasted_iota(jnp.int32, (tk, tk), 0)
    s_idx = lax.broadcasted_iota(jnp.int32, (tk, tk), 1)
    suffix_ones = (j_idx >= s_idx).astype(BF16)
    suffix_ones2 = jnp.concatenate([suffix_ones, suffix_ones], axis=0)
    t_loc = lax.broadcasted_iota(jnp.int32, (tq, tk), 0)
    s_loc = lax.broadcasted_iota(jnp.int32, (tq, tk), 1)
    hs = lambda h: slice(h * SB_DH, (h + 1) * SB_DH)

    def block(k0, mask, first):
        y, cs, carries = {}, {}, {}

        def scores(h):
            y[h] = _dot_nt(q_ref[:, hs(h)], k_ref[pl.ds(k0, tk), hs(h)]) * to_log2

        def suffix_sums(h):
            sp = jnp.maximum(y[h], 0.0) + jnp.log2(1.0 + jnp.exp2(-jnp.abs(y[h])))
            if mask is not None:
                sp = jnp.where(mask, sp, 0.0)
            hi, lo = _split_bf16(sp)
            cs[h] = _dot(jnp.concatenate([hi, lo], axis=1), suffix_ones2)

        def weighted_values(h):
            a = jnp.exp2(y.pop(h) - cs[h])
            if mask is not None:
                a = jnp.where(mask, a, 0.0)
            pv = _dot(a.astype(BF16), v_ref[pl.ds(k0, tk), hs(h)])
            row_sum = cs.pop(h)[:, 0:1]
            if first:
                acc_ref[:, hs(h)] = pv
                carries[h] = row_sum
            else:
                carry = carry_ref[h]
                acc_ref[:, hs(h)] += pv * jnp.exp2(-carry)
                carries[h] = carry + row_sum
            carry_ref[h] = carries[h]

        stages = (scores, suffix_sums, weighted_values)
        for step in range(heads + len(stages) - 1):
            for lag, stage in enumerate(stages):
                if 0 <= step - lag < heads:
                    stage(step - lag)
        return jnp.min(functools.reduce(jnp.minimum, [carries[h] for h in range(heads)]))

    n_diag = tq // tk
    cmin = None
    for d in range(n_diag):
        k0 = pl.multiple_of(q0 + (n_diag - 1 - d) * tk, tk)
        cmin = block(k0, (s_loc + k0) < (t_loc + q0), d == 0)

    def cond(state):
        kb, cmin = state
        return jnp.logical_and(kb >= 0, cmin < SB_SKIP_LOG2)

    def body(state):
        kb, _ = state
        return kb - 1, block(pl.multiple_of(kb * tk, tk), None, False)

    lax.while_loop(cond, body, (q0 // tk - 1, cmin))
    o_ref[...] = acc_ref[...].astype(o_ref.dtype)


def _sb_attn(z, weights, *, batch, seq, heads, q_off, k_off, v_off, tq, tk):
    m = z.shape[0]
    nq = seq // tq
    w = heads * SB_DH
    steps = batch * nq
    slab = lambda a: pl.BlockSpec((a.shape[0] // steps, a.shape[1]), lambda b, i: (b * nq + i, 0))
    for a in weights:
        assert a.shape[0] % (steps * BF16_SUBLANES) == 0, a.shape
    out = pl.pallas_call(
        functools.partial(_sb_kernel, tq=tq, tk=tk, heads=heads, n_cast=len(weights)),
        grid=(batch, nq),
        in_specs=[
            pl.BlockSpec((tq, w), lambda b, i: (b * nq + i, q_off)),
            pl.BlockSpec((seq, w), lambda b, i: (b, k_off)),
            pl.BlockSpec((seq, w), lambda b, i: (b, v_off)),
        ] + [slab(a) for a in weights],
        out_specs=[pl.BlockSpec((tq, w), lambda b, i: (b * nq + i, 0))] + [slab(a) for a in weights],
        out_shape=[jax.ShapeDtypeStruct((m, w), BF16)]
        + [jax.ShapeDtypeStruct(a.shape, BF16) for a in weights],
        scratch_shapes=[pltpu.VMEM((tq, w), F32), pltpu.VMEM((heads, tq, 1), F32)],
        compiler_params=pltpu.CompilerParams(
            dimension_semantics=("parallel", "arbitrary"),
            vmem_limit_bytes=VMEM_LIMIT),
        name="sb_attn",
    )(z, z, z, *weights)
    return out[0], out[1:]


def _mix_out_kernel(oa_ref, ob_ref, ga_ref, gb_ref, x_ref, wa_ref, wb_ref, wo_ref, g_ref, h_ref):
    ya = _dot(oa_ref[...], wa_ref[...])
    yb = _dot(ob_ref[...], wb_ref[...])
    y = _sigmoid(ga_ref[...].astype(F32)) * ya + _sigmoid(gb_ref[...].astype(F32)) * yb
    mix = _dot(y.astype(BF16), wo_ref[...])
    h_ref[...] = x_ref[...] + _rms(mix, g_ref[...])


def _resident(shape):
    return pl.BlockSpec(shape, lambda *_: (0,) * len(shape), pipeline_mode=pl.Buffered(1))


def _mix_out(o_a, o_b, z, x2, w_a, w_b, w_o, gain, *, ga_blk, gb_blk, tm):
    m, d = x2.shape
    return pl.pallas_call(
        _mix_out_kernel,
        grid=(m // tm,),
        in_specs=[
            pl.BlockSpec((tm, o_a.shape[1]), lambda i: (i, 0)),
            pl.BlockSpec((tm, o_b.shape[1]), lambda i: (i, 0)),
            pl.BlockSpec((tm, d), lambda i: (i, ga_blk)),
            pl.BlockSpec((tm, d), lambda i: (i, gb_blk)),
            pl.BlockSpec((tm, d), lambda i: (i, 0)),
            _resident(w_a.shape),
            _resident(w_b.shape),
            _resident(w_o.shape),
            _resident(gain.shape),
        ],
        out_specs=pl.BlockSpec((tm, d), lambda i: (i, 0)),
        out_shape=jax.ShapeDtypeStruct((m, d), F32),
        compiler_params=pltpu.CompilerParams(
            dimension_semantics=("parallel",),
            vmem_limit_bytes=VMEM_LIMIT),
        name="mix_out",
    )(o_a, o_b, z, z, x2, w_a, w_b, w_o, gain)


def _mlp_kernel(h_ref, gpre_ref, wu_ref, wd_ref, gpost_ref, o_ref, u_ref, acc_ref):
    f = pl.program_id(1)

    @pl.when(f == 0)
    def _():
        u_ref[...] = _rms(h_ref[...], gpre_ref[...]).astype(BF16)
        acc_ref[...] = jnp.zeros_like(acc_ref)

    a = jnp.maximum(_dot(u_ref[...], wu_ref[...]), 0.0)
    acc_ref[...] += _dot((a * a).astype(BF16), wd_ref[...])

    @pl.when(f == pl.num_programs(1) - 1)
    def _():
        o_ref[...] = h_ref[...] + _rms(acc_ref[...], gpost_ref[...])


def _mlp(h, g_pre, w_up, w_down, g_post, *, tm, tf):
    m, d = h.shape
    ff = w_up.shape[1]
    return pl.pallas_call(
        _mlp_kernel,
        grid=(m // tm, ff // tf),
        in_specs=[
            pl.BlockSpec((tm, d), lambda i, f: (i, 0)),
            pl.BlockSpec((1, d), lambda i, f: (0, 0)),
            pl.BlockSpec((d, tf), lambda i, f: (0, f)),
            pl.BlockSpec((tf, d), lambda i, f: (f, 0)),
            pl.BlockSpec((1, d), lambda i, f: (0, 0)),
        ],
        out_specs=pl.BlockSpec((tm, d), lambda i, f: (i, 0)),
        out_shape=jax.ShapeDtypeStruct((m, d), F32),
        scratch_shapes=[pltpu.VMEM((tm, d), BF16), pltpu.VMEM((tm, d), F32)],
        compiler_params=pltpu.CompilerParams(
            dimension_semantics=("parallel", "arbitrary"),
            vmem_limit_bytes=VMEM_LIMIT),
        name="mlp",
    )(h, g_pre, w_up, w_down, g_post)


def _ple_kernel(h_ref, p_ref, g_ref, wg_ref, wp_ref, o_ref):
    h = h_ref[...]
    e = _dot(p_ref[...].astype(BF16), wp_ref[...])
    gate = _sigmoid(_dot(_rms(h, g_ref[...]).astype(BF16), wg_ref[...]))
    o_ref[...] = h + gate * e


def _ple(h, p2, gain, w_gate, w_proj, *, tm):
    m, d = h.shape
    return pl.pallas_call(
        _ple_kernel,
        grid=(m // tm,),
        in_specs=[
            pl.BlockSpec((tm, d), lambda i: (i, 0)),
            pl.BlockSpec((tm, p2.shape[1]), lambda i: (i, 0)),
            _resident(gain.shape),
            _resident(w_gate.shape),
            _resident(w_proj.shape),
        ],
        out_specs=pl.BlockSpec((tm, d), lambda i: (i, 0)),
        out_shape=jax.ShapeDtypeStruct((m, d), F32),
        compiler_params=pltpu.CompilerParams(
            dimension_semantics=("parallel",),
            vmem_limit_bytes=VMEM_LIMIT),
        name="ple",
    )(h, p2, gain, w_gate, w_proj)


def _split_w_in(w_in):
    lr0 = 2 * _KEY_W + _VAL_W
    w_a = w_in[:, :lr0].astype(BF16)
    w_b = w_in[:, lr0 + GATE_RANK:].astype(BF16)
    w_lr = jnp.pad(w_in[:, lr0:lr0 + GATE_RANK], ((0, 0), (0, LANE - GATE_RANK))).astype(BF16)
    return w_a, w_b, w_lr


def kernel(x, p, norm_mix_pre, norm_mix_post, w_in, w_gate_up, b_gate, gla_norm, w_branch_gla,
           w_branch_sb, w_out, norm_mlp_pre, norm_mlp_post, w_mlp_up, w_mlp_down, norm_ple,
           w_ple_gate, w_ple_proj):
    batch, seq, d = x.shape
    depth = w_in.shape[0]
    m = batch * seq
    sb_w = d // 2
    sb_heads = sb_w // SB_DH
    h = x.reshape(m, d)
    row = lambda g: g.reshape(1, -1)
    for i in range(depth):
        w_a, w_b, w_lr = _split_w_in(w_in[i])
        w_gu = jnp.pad(w_gate_up[i], ((0, LANE - GATE_RANK), (0, 0))).astype(BF16)
        z, z_lr = _in_proj(h, row(norm_mix_pre[i]), w_a, w_b, w_lr, tm=1024, tn=1024)
        o_a = _gla(z, z_lr, w_gu, row(b_gate[i]), row(gla_norm[i]), batch=batch, seq=seq, rows=512)
        sq_off = (2 * _KEY_W + 2 * _VAL_W) // sb_w
        later_weights = (w_branch_gla[i], w_branch_sb[i], w_out[i], w_mlp_up[i], w_mlp_down[i],
                         w_ple_gate[i])
        o_b, (w_bg, w_bs, w_o, w_up, w_down, w_pg) = _sb_attn(
            z, later_weights, batch=batch, seq=seq, heads=sb_heads, q_off=sq_off,
            k_off=sq_off + 1, v_off=sq_off + 2, tq=256, tk=256)
        gate_blk = (2 * _KEY_W + 2 * _VAL_W + 3 * sb_w) // d
        h = _mix_out(o_a, o_b, z, h, w_bg, w_bs, w_o, row(norm_mix_post[i]),
                     ga_blk=gate_blk, gb_blk=gate_blk + 1, tm=512)
        h = _mlp(h, row(norm_mlp_pre[i]), w_up, w_down, row(norm_mlp_post[i]), tm=512, tf=1024)
        h = _ple(h, p[i].reshape(m, -1), row(norm_ple[i]), w_pg, w_ple_proj[i].astype(BF16), tm=512)
    return h.reshape(batch, seq, d)
```

```python
import functools

import jax
import jax.numpy as jnp
from jax import lax
from jax.experimental import pallas as pl
from jax.experimental.pallas import tpu as pltpu

F32 = jnp.float32
BF16 = jnp.bfloat16

EPS = 1e-6
GLA_HEADS = 4
GLA_DK = 128
GLA_DV = 256
GATE_RANK = 16
GATE_TAU = 16.0
GLA_CHUNK = 64
SB_DH = 128
LANE = 128
BF16_SUBLANES = 16

_KEY_W = GLA_HEADS * GLA_DK
_VAL_W = GLA_HEADS * GLA_DV

VMEM_LIMIT = 56 * 1024 * 1024


def _rms(x, gain):
    ms = jnp.mean(x * x, axis=-1, keepdims=True)
    return x * lax.rsqrt(ms + EPS) * gain


def _sigmoid(x):
    return 1.0 / (1.0 + jnp.exp(-x))


def _split_bf16(x):
    hi = x.astype(BF16)
    lo = (x - hi.astype(F32)).astype(BF16)
    return hi, lo


def _dot(a, b):
    return jnp.dot(a, b, preferred_element_type=F32)


def _dot_nt(a, b):
    return lax.dot_general(a, b, (((1,), (1,)), ((), ())), preferred_element_type=F32)


def _dot_tn(a, b):
    return lax.dot_general(a, b, (((0,), (0,)), ((), ())), preferred_element_type=F32)


def _in_proj_kernel(x_ref, g_ref, wa_ref, wb_ref, wlr_ref, z_ref, zlr_ref, u_ref, *, na):
    j = pl.program_id(1)

    @pl.when(j == 0)
    def _():
        u = _rms(x_ref[...], g_ref[...]).astype(BF16)
        u_ref[...] = u
        zlr_ref[...] = _dot(u, wlr_ref[...]).astype(zlr_ref.dtype)
        z_ref[...] = _dot(u, wa_ref[...]).astype(z_ref.dtype)

    @pl.when(jnp.logical_and(j > 0, j < na))
    def _():
        z_ref[...] = _dot(u_ref[...], wa_ref[...]).astype(z_ref.dtype)

    @pl.when(j >= na)
    def _():
        z_ref[...] = _dot(u_ref[...], wb_ref[...]).astype(z_ref.dtype)


def _in_proj(x2, gain, w_a, w_b, w_lr, *, tm, tn):
    m, d = x2.shape
    na, nb = w_a.shape[1] // tn, w_b.shape[1] // tn
    n = w_a.shape[1] + w_b.shape[1]
    return pl.pallas_call(
        functools.partial(_in_proj_kernel, na=na),
        grid=(m // tm, n // tn),
        in_specs=[
            pl.BlockSpec((tm, d), lambda i, j: (i, 0)),
            pl.BlockSpec((1, d), lambda i, j: (0, 0)),
            pl.BlockSpec((d, tn), lambda i, j: (0, jnp.where(j < na, j, 0))),
            pl.BlockSpec((d, tn), lambda i, j: (0, jnp.where(j < na, nb - 1, j - na))),
            pl.BlockSpec((d, LANE), lambda i, j: (0, 0)),
        ],
        out_specs=[
            pl.BlockSpec((tm, tn), lambda i, j: (i, j)),
            pl.BlockSpec((tm, LANE), lambda i, j: (i, 0)),
        ],
        out_shape=[
            jax.ShapeDtypeStruct((m, n), BF16),
            jax.ShapeDtypeStruct((m, LANE), BF16),
        ],
        scratch_shapes=[pltpu.VMEM((tm, d), BF16)],
        compiler_params=pltpu.CompilerParams(
            dimension_semantics=("parallel", "arbitrary"),
            vmem_limit_bytes=VMEM_LIMIT),
        name="in_proj",
    )(x2, gain, w_a, w_b, w_lr)


def _gla_kernel(q_ref, k_ref, v_ref, go_ref, lr_ref, wgu_ref, bg_ref, gn_ref,
                o_ref, s_ref, *, rows):
    @pl.when(pl.program_id(1) == 0)
    def _():
        s_ref[...] = jnp.zeros_like(s_ref)

    c = GLA_CHUNK
    nc = rows // c
    r_idx = lax.broadcasted_iota(jnp.int32, (c, c), 0)
    c_idx = lax.broadcasted_iota(jnp.int32, (c, c), 1)
    causal = r_idx >= c_idx
    tril = causal.astype(BF16)
    tril2 = jnp.concatenate([tril, tril], axis=1)
    scale = GLA_DK ** -0.5
    gn = gn_ref[...]
    ksl = lambda h: slice(h * GLA_DK, (h + 1) * GLA_DK)
    vsl = lambda h: slice(h * GLA_DV, (h + 1) * GLA_DV)

    pre = _dot(lr_ref[...], wgu_ref[...]) + bg_ref[...]
    la = (jnp.minimum(pre, 0.0) - jnp.log(1.0 + jnp.exp(-jnp.abs(pre)))) * (1.0 / GATE_TAU)
    la_hi, la_lo = _split_bf16(la)

    q_dec, k_intra, k_state, decay = [], [], [], []
    for ci in range(nc):
        rs = slice(ci * c, (ci + 1) * c)
        b = _dot(tril2, jnp.concatenate([la_hi[rs], la_lo[rs]], axis=0))
        b_last = b[c - 1:c, :]
        q = q_ref[rs, :].astype(F32)
        k = k_ref[rs, :].astype(F32)
        q_dec.append((q * scale * jnp.exp(b)).astype(BF16))
        k_intra.append((k * jnp.exp(-b)).astype(BF16))
        k_state.append((k * jnp.exp(b_last - b)).astype(BF16))
        decay.append(jnp.exp(b_last))

    def intra(ci):
        rs = slice(ci * c, (ci + 1) * c)
        out = []
        for h in range(GLA_HEADS):
            v_h = v_ref[rs, vsl(h)]
            scores = _dot_nt(q_dec[ci][:, ksl(h)], k_intra[ci][:, ksl(h)])
            scores = jnp.where(causal, scores, 0.0).astype(BF16)
            out.append((_dot(scores, v_h), _dot_tn(v_h, k_state[ci][:, ksl(h)])))
        return out

    def inter(ci, intra_ci):
        rs = slice(ci * c, (ci + 1) * c)
        for h in range(GLA_HEADS):
            o_intra, kv = intra_ci[h]
            st = s_ref[h]
            o = o_intra + _dot_nt(q_dec[ci][:, ksl(h)], st.astype(BF16))
            s_ref[h] = decay[ci][:, ksl(h)] * st + kv
            o = _rms(o, gn)
            go = go_ref[rs, vsl(h)].astype(F32)
            o_ref[rs, vsl(h)] = (o * (go * _sigmoid(go))).astype(o_ref.dtype)

    ahead = intra(0)
    for ci in range(nc):
        current, ahead = ahead, (intra(ci + 1) if ci + 1 < nc else None)
        inter(ci, current)


def _gla(z, z_lr, w_gu, b_gate, gla_norm, *, batch, seq, rows):
    m = z.shape[0]
    nb = seq // rows
    kw, vw = _KEY_W, _VAL_W
    row = lambda b, r: b * nb + r
    return pl.pallas_call(
        functools.partial(_gla_kernel, rows=rows),
        grid=(batch, nb),
        in_specs=[
            pl.BlockSpec((rows, kw), lambda b, r: (row(b, r), 0)),
            pl.BlockSpec((rows, kw), lambda b, r: (row(b, r), 1)),
            pl.BlockSpec((rows, vw), lambda b, r: (row(b, r), 1)),
            pl.BlockSpec((rows, vw), lambda b, r: (row(b, r), 2)),
            pl.BlockSpec((rows, LANE), lambda b, r: (row(b, r), 0)),
            pl.BlockSpec((LANE, kw), lambda b, r: (0, 0)),
            pl.BlockSpec((1, kw), lambda b, r: (0, 0)),
            pl.BlockSpec((1, GLA_DV), lambda b, r: (0, 0)),
        ],
        out_specs=pl.BlockSpec((rows, vw), lambda b, r: (row(b, r), 0)),
        out_shape=jax.ShapeDtypeStruct((m, vw), BF16),
        scratch_shapes=[pltpu.VMEM((GLA_HEADS, GLA_DV, GLA_DK), F32)],
        compiler_params=pltpu.CompilerParams(
            dimension_semantics=("parallel", "arbitrary"),
            vmem_limit_bytes=VMEM_LIMIT),
        name="gla",
    )(z, z, z, z, z_lr, w_gu, b_gate, gla_norm)


SB_SKIP_LOG2 = 151.0
LOG2_E = 1.4426950408889634


def _sb_kernel(q_ref, k_ref, v_ref, *rest, tq, tk, heads, n_cast):
    w32_refs, o_ref, w16_refs = rest[:n_cast], rest[n_cast], rest[n_cast + 1:2 * n_cast + 1]
    acc_ref, carry_ref = rest[2 * n_cast + 1:]
    for w32, w16 in zip(w32_refs, w16_refs):
        w16[...] = w32[...].astype(w16.dtype)

    q0 = pl.program_id(1) * tq
    to_log2 = (SB_DH ** -0.5) * LOG2_E
    j_idx = lax.broadcasted_iota(jnp.int32, (tk, tk), 0)
    s_idx = lax.broadcasted_iota(jnp.int32, (tk, tk), 1)
    suffix_ones = (j_idx >= s_idx).astype(BF16)
    suffix_ones2 = jnp.concatenate([suffix_ones, suffix_ones], axis=0)
    t_loc = lax.broadcasted_iota(jnp.int32, (tq, tk), 0)
    s_loc = lax.broadcasted_iota(jnp.int32, (tq, tk), 1)
    hs = lambda h: slice(h * SB_DH, (h + 1) * SB_DH)

    def block(k0, mask, first):
        y, cs, carries = {}, {}, {}

        def scores(h):
            y[h] = _dot_nt(q_ref[:, hs(h)], k_ref[pl.ds(k0, tk), hs(h)]) * to_log2

        def suffix_sums(h):
            sp = jnp.maximum(y[h], 0.0) + jnp.log2(1.0 + jnp.exp2(-jnp.abs(y[h])))
            if mask is not None:
                sp = jnp.where(mask, sp, 0.0)
            hi, lo = _split_bf16(sp)
            cs[h] = _dot(jnp.concatenate([hi, lo], axis=1), suffix_ones2)

        def weighted_values(h):
            a = jnp.exp2(y.pop(h) - cs[h])
            if mask is not None:
                a = jnp.where(mask, a, 0.0)
            pv = _dot(a.astype(BF16), v_ref[pl.ds(k0, tk), hs(h)])
            row_sum = cs.pop(h)[:, 0:1]
            if first:
                acc_ref[:, hs(h)] = pv
                carries[h] = row_sum
            else:
                carry = carry_ref[h]
                acc_ref[:, hs(h)] += pv * jnp.exp2(-carry)
                carries[h] = carry + row_sum
            carry_ref[h] = carries[h]

        stages = (scores, suffix_sums, weighted_values)
        for step in range(heads + len(stages) - 1):
            for lag, stage in enumerate(stages):
                if 0 <= step - lag < heads:
                    stage(step - lag)
        return jnp.min(functools.reduce(jnp.minimum, [carries[h] for h in range(heads)]))

    n_diag = tq // tk
    cmin = None
    for d in range(n_diag):
        k0 = pl.multiple_of(q0 + (n_diag - 1 - d) * tk, tk)
        cmin = block(k0, (s_loc + k0) < (t_loc + q0), d == 0)

    def cond(state):
        kb, cmin = state
        return jnp.logical_and(kb >= 0, cmin < SB_SKIP_LOG2)

    def body(state):
        kb, _ = state
        return kb - 1, block(pl.multiple_of(kb * tk, tk), None, False)

    lax.while_loop(cond, body, (q0 // tk - 1, cmin))
    o_ref[...] = acc_ref[...].astype(o_ref.dtype)


def _sb_attn(z, weights, *, batch, seq, heads, q_off, k_off, v_off, tq, tk):
    m = z.shape[0]
    nq = seq // tq
    w = heads * SB_DH
    steps = batch * nq
    slab = lambda a: pl.BlockSpec((a.shape[0] // steps, a.shape[1]), lambda b, i: (b * nq + i, 0))
    for a in weights:
        assert a.shape[0] % (steps * BF16_SUBLANES) == 0, a.shape
    out = pl.pallas_call(
        functools.partial(_sb_kernel, tq=tq, tk=tk, heads=heads, n_cast=len(weights)),
        grid=(batch, nq),
        in_specs=[
            pl.BlockSpec((tq, w), lambda b, i: (b * nq + i, q_off)),
            pl.BlockSpec((seq, w), lambda b, i: (b, k_off)),
            pl.BlockSpec((seq, w), lambda b, i: (b, v_off)),
        ] + [slab(a) for a in weights],
        out_specs=[pl.BlockSpec((tq, w), lambda b, i: (b * nq + i, 0))] + [slab(a) for a in weights],
        out_shape=[jax.ShapeDtypeStruct((m, w), BF16)]
        + [jax.ShapeDtypeStruct(a.shape, BF16) for a in weights],
        scratch_shapes=[pltpu.VMEM((tq, w), F32), pltpu.VMEM((heads, tq, 1), F32)],
        compiler_params=pltpu.CompilerParams(
            dimension_semantics=("parallel", "arbitrary"),
            vmem_limit_bytes=VMEM_LIMIT),
        name="sb_attn",
    )(z, z, z, *weights)
    return out[0], out[1:]


def _mix_out_kernel(oa_ref, ob_ref, ga_ref, gb_ref, x_ref, wa_ref, wb_ref, wo_ref, g_ref, h_ref):
    ya = _dot(oa_ref[...], wa_ref[...])
    yb = _dot(ob_ref[...], wb_ref[...])
    y = _sigmoid(ga_ref[...].astype(F32)) * ya + _sigmoid(gb_ref[...].astype(F32)) * yb
    mix = _dot(y.astype(BF16), wo_ref[...])
    h_ref[...] = x_ref[...] + _rms(mix, g_ref[...])


def _resident(shape):
    return pl.BlockSpec(shape, lambda *_: (0,) * len(shape), pipeline_mode=pl.Buffered(1))


def _mix_out(o_a, o_b, z, x2, w_a, w_b, w_o, gain, *, ga_blk, gb_blk, tm):
    m, d = x2.shape
    return pl.pallas_call(
        _mix_out_kernel,
        grid=(m // tm,),
        in_specs=[
            pl.BlockSpec((tm, o_a.shape[1]), lambda i: (i, 0)),
            pl.BlockSpec((tm, o_b.shape[1]), lambda i: (i, 0)),
            pl.BlockSpec((tm, d), lambda i: (i, ga_blk)),
            pl.BlockSpec((tm, d), lambda i: (i, gb_blk)),
            pl.BlockSpec((tm, d), lambda i: (i, 0)),
            _resident(w_a.shape),
            _resident(w_b.shape),
            _resident(w_o.shape),
            _resident(gain.shape),
        ],
        out_specs=pl.BlockSpec((tm, d), lambda i: (i, 0)),
        out_shape=jax.ShapeDtypeStruct((m, d), F32),
        compiler_params=pltpu.CompilerParams(
            dimension_semantics=("parallel",),
            vmem_limit_bytes=VMEM_LIMIT),
        name="mix_out",
    )(o_a, o_b, z, z, x2, w_a, w_b, w_o, gain)


def _mlp_kernel(h_ref, gpre_ref, wu_ref, wd_ref, gpost_ref, o_ref, u_ref, acc_ref):
    f = pl.program_id(1)
    last = pl.num_programs(1) - 1

    def ffn(u):
        a = jnp.maximum(_dot(u, wu_ref[...]), 0.0)
        return _dot((a * a).astype(BF16), wd_ref[...])

    @pl.when(f == 0)
    def _():
        u = _rms(h_ref[...], gpre_ref[...]).astype(BF16)
        u_ref[...] = u
        acc_ref[...] = ffn(u)

    @pl.when(jnp.logical_and(f > 0, f < last))
    def _():
        acc_ref[...] += ffn(u_ref[...])

    @pl.when(f == last)
    def _():
        total = acc_ref[...] + ffn(u_ref[...])
        o_ref[...] = h_ref[...] + _rms(total, gpost_ref[...])


def _mlp(h, g_pre, w_up, w_down, g_post, *, tm, tf):
    m, d = h.shape
    ff = w_up.shape[1]
    assert ff // tf >= 2, "the first and last hidden tiles take different branches"
    return pl.pallas_call(
        _mlp_kernel,
        grid=(m // tm, ff // tf),
        in_specs=[
            pl.BlockSpec((tm, d), lambda i, f: (i, 0)),
            pl.BlockSpec((1, d), lambda i, f: (0, 0)),
            pl.BlockSpec((d, tf), lambda i, f: (0, f)),
            pl.BlockSpec((tf, d), lambda i, f: (f, 0)),
            pl.BlockSpec((1, d), lambda i, f: (0, 0)),
        ],
        out_specs=pl.BlockSpec((tm, d), lambda i, f: (i, 0)),
        out_shape=jax.ShapeDtypeStruct((m, d), F32),
        scratch_shapes=[pltpu.VMEM((tm, d), BF16), pltpu.VMEM((tm, d), F32)],
        compiler_params=pltpu.CompilerParams(
            dimension_semantics=("parallel", "arbitrary"),
            vmem_limit_bytes=VMEM_LIMIT),
        name="mlp",
    )(h, g_pre, w_up, w_down, g_post)


def _ple_kernel(h_ref, p_ref, g_ref, wg_ref, wp_ref, o_ref):
    h = h_ref[...]
    e = _dot(p_ref[...].astype(BF16), wp_ref[...])
    gate = _sigmoid(_dot(_rms(h, g_ref[...]).astype(BF16), wg_ref[...]))
    o_ref[...] = h + gate * e


def _ple(h, p2, gain, w_gate, w_proj, *, tm):
    m, d = h.shape
    return pl.pallas_call(
        _ple_kernel,
        grid=(m // tm,),
        in_specs=[
            pl.BlockSpec((tm, d), lambda i: (i, 0)),
            pl.BlockSpec((tm, p2.shape[1]), lambda i: (i, 0)),
            _resident(gain.shape),
            _resident(w_gate.shape),
            _resident(w_proj.shape),
        ],
        out_specs=pl.BlockSpec((tm, d), lambda i: (i, 0)),
        out_shape=jax.ShapeDtypeStruct((m, d), F32),
        compiler_params=pltpu.CompilerParams(
            dimension_semantics=("parallel",),
            vmem_limit_bytes=VMEM_LIMIT),
        name="ple",
    )(h, p2, gain, w_gate, w_proj)


def _split_w_in(w_in):
    lr0 = 2 * _KEY_W + _VAL_W
    w_a = w_in[:, :lr0].astype(BF16)
    w_b = w_in[:, lr0 + GATE_RANK:].astype(BF16)
    w_lr = jnp.pad(w_in[:, lr0:lr0 + GATE_RANK], ((0, 0), (0, LANE - GATE_RANK))).astype(BF16)
    return w_a, w_b, w_lr


def kernel(x, p, norm_mix_pre, norm_mix_post, w_in, w_gate_up, b_gate, gla_norm, w_branch_gla,
           w_branch_sb, w_out, norm_mlp_pre, norm_mlp_post, w_mlp_up, w_mlp_down, norm_ple,
           w_ple_gate, w_ple_proj):
    batch, seq, d = x.shape
    depth = w_in.shape[0]
    m = batch * seq
    sb_w = d // 2
    sb_heads = sb_w // SB_DH
    h = x.reshape(m, d)
    row = lambda g: g.reshape(1, -1)
    for i in range(depth):
        w_a, w_b, w_lr = _split_w_in(w_in[i])
        w_gu = jnp.pad(w_gate_up[i], ((0, LANE - GATE_RANK), (0, 0))).astype(BF16)
        z, z_lr = _in_proj(h, row(norm_mix_pre[i]), w_a, w_b, w_lr, tm=1024, tn=1024)
        o_a = _gla(z, z_lr, w_gu, row(b_gate[i]), row(gla_norm[i]), batch=batch, seq=seq, rows=512)
        sq_off = (2 * _KEY_W + 2 * _VAL_W) // sb_w
        later_weights = (w_branch_gla[i], w_branch_sb[i], w_out[i], w_mlp_up[i], w_mlp_down[i],
                         w_ple_gate[i])
        o_b, (w_bg, w_bs, w_o, w_up, w_down, w_pg) = _sb_attn(
            z, later_weights, batch=batch, seq=seq, heads=sb_heads, q_off=sq_off,
            k_off=sq_off + 1, v_off=sq_off + 2, tq=256, tk=256)
        gate_blk = (2 * _KEY_W + 2 * _VAL_W + 3 * sb_w) // d
        h = _mix_out(o_a, o_b, z, h, w_bg, w_bs, w_o, row(norm_mix_post[i]),
                     ga_blk=gate_blk, gb_blk=gate_blk + 1, tm=512)
        h = _mlp(h, row(norm_mlp_pre[i]), w_up, w_down, row(norm_mlp_post[i]), tm=512, tf=1024)
        h = _ple(h, p[i].reshape(m, -1), row(norm_ple[i]), w_pg, w_ple_proj[i].astype(BF16), tm=512)
    return h.reshape(batch, seq, d)
```

```python
import functools

import jax
import jax.numpy as jnp
from jax import lax
from jax.experimental import pallas as pl
from jax.experimental.pallas import tpu as pltpu

F32 = jnp.float32
BF16 = jnp.bfloat16

EPS = 1e-6
GLA_HEADS = 4
GLA_DK = 128
GLA_DV = 256
GATE_RANK = 16
GATE_TAU = 16.0
GLA_CHUNK = 64
SB_DH = 128
LANE = 128
BF16_SUBLANES = 16

_KEY_W = GLA_HEADS * GLA_DK
_VAL_W = GLA_HEADS * GLA_DV

VMEM_LIMIT = 56 * 1024 * 1024


def _rms(x, gain):
    ms = jnp.mean(x * x, axis=-1, keepdims=True)
    return x * lax.rsqrt(ms + EPS) * gain


def _sigmoid(x):
    return 1.0 / (1.0 + jnp.exp(-x))


def _split_bf16(x):
    hi = x.astype(BF16)
    lo = (x - hi.astype(F32)).astype(BF16)
    return hi, lo


def _dot(a, b):
    return jnp.dot(a, b, preferred_element_type=F32)


def _dot_nt(a, b):
    return lax.dot_general(a, b, (((1,), (1,)), ((), ())), preferred_element_type=F32)


def _dot_tn(a, b):
    return lax.dot_general(a, b, (((0,), (0,)), ((), ())), preferred_element_type=F32)


def _head_cast_kernel(wt_ref, wa_ref, wlr_ref, *, n_blocks, rank):
    r = pl.program_id(0)
    w = wt_ref[...].T

    @pl.when(r < n_blocks)
    def _():
        wa_ref[...] = w.astype(wa_ref.dtype)

    @pl.when(r == n_blocks)
    def _():
        lr = w[:, :LANE]
        col = lax.broadcasted_iota(jnp.int32, lr.shape, 1)
        wlr_ref[...] = jnp.where(col < rank, lr, 0.0).astype(wlr_ref.dtype)


def _head_cast(w_t, *, n_a, rank, block):
    d = w_t.shape[1]
    n_blocks = n_a // block
    return pl.pallas_call(
        functools.partial(_head_cast_kernel, n_blocks=n_blocks, rank=rank),
        grid=(n_blocks + 1,),
        in_specs=[pl.BlockSpec((block, d), lambda r: (r, 0))],
        out_specs=[pl.BlockSpec((d, block), lambda r: (0, jnp.minimum(r, n_blocks - 1))),
                   pl.BlockSpec((d, LANE), lambda r: (0, 0))],
        out_shape=[jax.ShapeDtypeStruct((d, n_a), BF16), jax.ShapeDtypeStruct((d, LANE), BF16)],
        compiler_params=pltpu.CompilerParams(dimension_semantics=("arbitrary",)),
        name="w_in_head_cast",
    )(w_t)


def _in_proj_a_kernel(x_ref, g_ref, wa_ref, wlr_ref, wrows_ref, za_ref, zlr_ref, u_ref, wb_ref):
    u = _rms(x_ref[...], g_ref[...]).astype(BF16)
    u_ref[...] = u
    zlr_ref[...] = _dot(u, wlr_ref[...]).astype(zlr_ref.dtype)
    za_ref[...] = _dot(u, wa_ref[...]).astype(za_ref.dtype)
    wb_ref[...] = wrows_ref[...].T.astype(wb_ref.dtype)


def _in_proj_a(x2, gain, w_a, w_lr, w_t, *, row_lo, tm):
    m, d = x2.shape
    steps = m // tm
    n_a = w_a.shape[1]
    n_b = w_t.shape[0] - row_lo
    slab = n_b // steps
    assert slab * steps == n_b and slab % LANE == 0 and row_lo % BF16_SUBLANES == 0
    return pl.pallas_call(
        _in_proj_a_kernel,
        grid=(steps,),
        in_specs=[
            pl.BlockSpec((tm, d), lambda i: (i, 0)),
            _resident(gain.shape),
            _resident(w_a.shape),
            _resident(w_lr.shape),
            pl.BlockSpec((pl.Element(slab), pl.Element(d)),
                         lambda i: (pl.multiple_of(row_lo + i * slab, BF16_SUBLANES), 0)),
        ],
        out_specs=[
            pl.BlockSpec((tm, n_a), lambda i: (i, 0)),
            pl.BlockSpec((tm, LANE), lambda i: (i, 0)),
            pl.BlockSpec((tm, d), lambda i: (i, 0)),
            pl.BlockSpec((d, slab), lambda i: (0, i)),
        ],
        out_shape=[
            jax.ShapeDtypeStruct((m, n_a), BF16),
            jax.ShapeDtypeStruct((m, LANE), BF16),
            jax.ShapeDtypeStruct((m, d), BF16),
            jax.ShapeDtypeStruct((d, n_b), BF16),
        ],
        compiler_params=pltpu.CompilerParams(
            dimension_semantics=("parallel",),
            vmem_limit_bytes=VMEM_LIMIT),
        name="in_proj_a",
    )(x2, gain, w_a, w_lr, w_t)


def _in_proj_b_kernel(u_ref, w_ref, z_ref):
    z_ref[...] = _dot(u_ref[...], w_ref[...]).astype(z_ref.dtype)


def _in_proj_b(u, w_b, *, tm, tn):
    m, d = u.shape
    n = w_b.shape[1]
    return pl.pallas_call(
        _in_proj_b_kernel,
        grid=(m // tm, n // tn),
        in_specs=[
            pl.BlockSpec((tm, d), lambda i, j: (i, 0)),
            pl.BlockSpec((d, tn), lambda i, j: (0, j)),
        ],
        out_specs=pl.BlockSpec((tm, tn), lambda i, j: (i, j)),
        out_shape=jax.ShapeDtypeStruct((m, n), BF16),
        compiler_params=pltpu.CompilerParams(
            dimension_semantics=("parallel", "arbitrary"),
            vmem_limit_bytes=VMEM_LIMIT),
        name="in_proj_b",
    )(u, w_b)


def _gla_kernel(q_ref, k_ref, v_ref, go_ref, lr_ref, wgu_ref, bg_ref, gn_ref,
                o_ref, s_ref, *, rows):
    @pl.when(pl.program_id(1) == 0)
    def _():
        s_ref[...] = jnp.zeros_like(s_ref)

    c = GLA_CHUNK
    nc = rows // c
    r_idx = lax.broadcasted_iota(jnp.int32, (c, c), 0)
    c_idx = lax.broadcasted_iota(jnp.int32, (c, c), 1)
    causal = r_idx >= c_idx
    tril = causal.astype(BF16)
    tril2 = jnp.concatenate([tril, tril], axis=1)
    scale = GLA_DK ** -0.5
    gn = gn_ref[...]
    ksl = lambda h: slice(h * GLA_DK, (h + 1) * GLA_DK)
    vsl = lambda h: slice(h * GLA_DV, (h + 1) * GLA_DV)

    pre = _dot(lr_ref[...], wgu_ref[...]) + bg_ref[...]
    la = (jnp.minimum(pre, 0.0) - jnp.log(1.0 + jnp.exp(-jnp.abs(pre)))) * (1.0 / GATE_TAU)
    la_hi, la_lo = _split_bf16(la)

    q_dec, k_intra, k_state, decay = [], [], [], []
    for ci in range(nc):
        rs = slice(ci * c, (ci + 1) * c)
        b = _dot(tril2, jnp.concatenate([la_hi[rs], la_lo[rs]], axis=0))
        b_last = b[c - 1:c, :]
        q = q_ref[rs, :].astype(F32)
        k = k_ref[rs, :].astype(F32)
        q_dec.append((q * scale * jnp.exp(b)).astype(BF16))
        k_intra.append((k * jnp.exp(-b)).astype(BF16))
        k_state.append((k * jnp.exp(b_last - b)).astype(BF16))
        decay.append(jnp.exp(b_last))

    def intra(ci):
        rs = slice(ci * c, (ci + 1) * c)
        out = []
        for h in range(GLA_HEADS):
            v_h = v_ref[rs, vsl(h)]
            scores = _dot_nt(q_dec[ci][:, ksl(h)], k_intra[ci][:, ksl(h)])
            scores = jnp.where(causal, scores, 0.0).astype(BF16)
            out.append((_dot(scores, v_h), _dot_tn(v_h, k_state[ci][:, ksl(h)])))
        return out

    def inter(ci, intra_ci):
        rs = slice(ci * c, (ci + 1) * c)
        for h in range(GLA_HEADS):
            o_intra, kv = intra_ci[h]
            st = s_ref[h]
            o = o_intra + _dot_nt(q_dec[ci][:, ksl(h)], st.astype(BF16))
            s_ref[h] = decay[ci][:, ksl(h)] * st + kv
            o = _rms(o, gn)
            go = go_ref[rs, vsl(h)].astype(F32)
            o_ref[rs, vsl(h)] = (o * (go * _sigmoid(go))).astype(o_ref.dtype)

    ahead = intra(0)
    for ci in range(nc):
        current, ahead = ahead, (intra(ci + 1) if ci + 1 < nc else None)
        inter(ci, current)


def _gla(z_a, z_b, z_lr, w_gu, b_gate, gla_norm, *, batch, seq, rows):
    m = z_a.shape[0]
    nb = seq // rows
    kw, vw = _KEY_W, _VAL_W
    row = lambda b, r: b * nb + r
    return pl.pallas_call(
        functools.partial(_gla_kernel, rows=rows),
        grid=(batch, nb),
        in_specs=[
            pl.BlockSpec((rows, kw), lambda b, r: (row(b, r), 0)),
            pl.BlockSpec((rows, kw), lambda b, r: (row(b, r), 1)),
            pl.BlockSpec((rows, vw), lambda b, r: (row(b, r), 1)),
            pl.BlockSpec((rows, vw), lambda b, r: (row(b, r), 0)),
            pl.BlockSpec((rows, LANE), lambda b, r: (row(b, r), 0)),
            pl.BlockSpec((LANE, kw), lambda b, r: (0, 0)),
            pl.BlockSpec((1, kw), lambda b, r: (0, 0)),
            pl.BlockSpec((1, GLA_DV), lambda b, r: (0, 0)),
        ],
        out_specs=pl.BlockSpec((rows, vw), lambda b, r: (row(b, r), 0)),
        out_shape=jax.ShapeDtypeStruct((m, vw), BF16),
        scratch_shapes=[pltpu.VMEM((GLA_HEADS, GLA_DV, GLA_DK), F32)],
        compiler_params=pltpu.CompilerParams(
            dimension_semantics=("parallel", "arbitrary"),
            vmem_limit_bytes=VMEM_LIMIT),
        name="gla",
    )(z_a, z_a, z_a, z_b, z_lr, w_gu, b_gate, gla_norm)


SB_SKIP_LOG2 = 151.0
LOG2_E = 1.4426950408889634


def _sb_kernel(q_ref, k_ref, v_ref, *rest, tq, tk, heads, n_cast):
    w32_refs, o_ref, w16_refs = rest[:n_cast], rest[n_cast], rest[n_cast + 1:2 * n_cast + 1]
    acc_ref, carry_ref = rest[2 * n_cast + 1:]
    for w32, w16 in zip(w32_refs, w16_refs):
        w16[...] = w32[...].astype(w16.dtype)

    q0 = pl.program_id(1) * tq
    to_log2 = (SB_DH ** -0.5) * LOG2_E
    j_idx = lax.broadcasted_iota(jnp.int32, (tk, tk), 0)
    s_idx = lax.broadcasted_iota(jnp.int32, (tk, tk), 1)
    suffix_ones = (j_idx >= s_idx).astype(BF16)
    suffix_ones2 = jnp.concatenate([suffix_ones, suffix_ones], axis=0)
    t_loc = lax.broadcasted_iota(jnp.int32, (tq, tk), 0)
    s_loc = lax.broadcasted_iota(jnp.int32, (tq, tk), 1)
    hs = lambda h: slice(h * SB_DH, (h + 1) * SB_DH)

    def block(k0, mask, first):
        y, cs, carries = {}, {}, {}

        def scores(h):
            y[h] = _dot_nt(q_ref[:, hs(h)], k_ref[pl.ds(k0, tk), hs(h)]) * to_log2

        def suffix_sums(h):
            sp = jnp.maximum(y[h], 0.0) + jnp.log2(1.0 + jnp.exp2(-jnp.abs(y[h])))
            if mask is not None:
                sp = jnp.where(mask, sp, 0.0)
            hi, lo = _split_bf16(sp)
            cs[h] = _dot(jnp.concatenate([hi, lo], axis=1), suffix_ones2)

        def weighted_values(h):
            a = jnp.exp2(y.pop(h) - cs[h])
            if mask is not None:
                a = jnp.where(mask, a, 0.0)
            pv = _dot(a.astype(BF16), v_ref[pl.ds(k0, tk), hs(h)])
            row_sum = cs.pop(h)[:, 0:1]
            if first:
                acc_ref[:, hs(h)] = pv
                carries[h] = row_sum
            else:
                carry = carry_ref[h]
                acc_ref[:, hs(h)] += pv * jnp.exp2(-carry)
                carries[h] = carry + row_sum
            carry_ref[h] = carries[h]

        stages = (scores, suffix_sums, weighted_values)
        for step in range(heads + len(stages) - 1):
            for lag, stage in enumerate(stages):
                if 0 <= step - lag < heads:
                    stage(step - lag)
        return jnp.min(functools.reduce(jnp.minimum, [carries[h] for h in range(heads)]))

    n_diag = tq // tk
    cmin = None
    for d in range(n_diag):
        k0 = pl.multiple_of(q0 + (n_diag - 1 - d) * tk, tk)
        cmin = block(k0, (s_loc + k0) < (t_loc + q0), d == 0)

    def cond(state):
        kb, cmin = state
        return jnp.logical_and(kb >= 0, cmin < SB_SKIP_LOG2)

    def body(state):
        kb, _ = state
        return kb - 1, block(pl.multiple_of(kb * tk, tk), None, False)

    lax.while_loop(cond, body, (q0 // tk - 1, cmin))
    o_ref[...] = acc_ref[...].astype(o_ref.dtype)


def _sb_attn(z, weights, *, batch, seq, heads, q_off, k_off, v_off, tq, tk):
    m = z.shape[0]
    nq = seq // tq
    w = heads * SB_DH
    steps = batch * nq
    slab = lambda a: pl.BlockSpec((a.shape[0] // steps, a.shape[1]), lambda b, i: (b * nq + i, 0))
    for a in weights:
        assert a.shape[0] % (steps * BF16_SUBLANES) == 0, a.shape
    out = pl.pallas_call(
        functools.partial(_sb_kernel, tq=tq, tk=tk, heads=heads, n_cast=len(weights)),
        grid=(batch, nq),
        in_specs=[
            pl.BlockSpec((tq, w), lambda b, i: (b * nq + i, q_off)),
            pl.BlockSpec((seq, w), lambda b, i: (b, k_off)),
            pl.BlockSpec((seq, w), lambda b, i: (b, v_off)),
        ] + [slab(a) for a in weights],
        out_specs=[pl.BlockSpec((tq, w), lambda b, i: (b * nq + i, 0))] + [slab(a) for a in weights],
        out_shape=[jax.ShapeDtypeStruct((m, w), BF16)]
        + [jax.ShapeDtypeStruct(a.shape, BF16) for a in weights],
        scratch_shapes=[pltpu.VMEM((tq, w), F32), pltpu.VMEM((heads, tq, 1), F32)],
        compiler_params=pltpu.CompilerParams(
            dimension_semantics=("parallel", "arbitrary"),
            vmem_limit_bytes=VMEM_LIMIT),
        name="sb_attn",
    )(z, z, z, *weights)
    return out[0], out[1:]


def _mix_out_kernel(oa_ref, ob_ref, ga_ref, gb_ref, x_ref, wa_ref, wb_ref, wo_ref, g_ref, h_ref):
    ya = _dot(oa_ref[...], wa_ref[...])
    yb = _dot(ob_ref[...], wb_ref[...])
    y = _sigmoid(ga_ref[...].astype(F32)) * ya + _sigmoid(gb_ref[...].astype(F32)) * yb
    mix = _dot(y.astype(BF16), wo_ref[...])
    h_ref[...] = x_ref[...] + _rms(mix, g_ref[...])


def _resident(shape):
    return pl.BlockSpec(shape, lambda *_: (0,) * len(shape), pipeline_mode=pl.Buffered(1))


def _mix_out(o_a, o_b, z, x2, w_a, w_b, w_o, gain, *, ga_blk, gb_blk, tm):
    m, d = x2.shape
    return pl.pallas_call(
        _mix_out_kernel,
        grid=(m // tm,),
        in_specs=[
            pl.BlockSpec((tm, o_a.shape[1]), lambda i: (i, 0)),
            pl.BlockSpec((tm, o_b.shape[1]), lambda i: (i, 0)),
            pl.BlockSpec((tm, d), lambda i: (i, ga_blk)),
            pl.BlockSpec((tm, d), lambda i: (i, gb_blk)),
            pl.BlockSpec((tm, d), lambda i: (i, 0)),
            _resident(w_a.shape),
            _resident(w_b.shape),
            _resident(w_o.shape),
            _resident(gain.shape),
        ],
        out_specs=pl.BlockSpec((tm, d), lambda i: (i, 0)),
        out_shape=jax.ShapeDtypeStruct((m, d), F32),
        compiler_params=pltpu.CompilerParams(
            dimension_semantics=("parallel",),
            vmem_limit_bytes=VMEM_LIMIT),
        name="mix_out",
    )(o_a, o_b, z, z, x2, w_a, w_b, w_o, gain)


def _mlp_kernel(h_ref, gpre_ref, wu_ref, wd_ref, gpost_ref, o_ref, u_ref, acc_ref):
    f = pl.program_id(1)
    last = pl.num_programs(1) - 1

    def ffn(u):
        a = jnp.maximum(_dot(u, wu_ref[...]), 0.0)
        return _dot((a * a).astype(BF16), wd_ref[...])

    @pl.when(f == 0)
    def _():
        u = _rms(h_ref[...], gpre_ref[...]).astype(BF16)
        u_ref[...] = u
        acc_ref[...] = ffn(u)

    @pl.when(jnp.logical_and(f > 0, f < last))
    def _():
        acc_ref[...] += ffn(u_ref[...])

    @pl.when(f == last)
    def _():
        total = acc_ref[...] + ffn(u_ref[...])
        o_ref[...] = h_ref[...] + _rms(total, gpost_ref[...])


def _mlp(h, g_pre, w_up, w_down, g_post, *, tm, tf):
    m, d = h.shape
    ff = w_up.shape[1]
    assert ff // tf >= 2, "the first and last hidden tiles take different branches"
    return pl.pallas_call(
        _mlp_kernel,
        grid=(m // tm, ff // tf),
        in_specs=[
            pl.BlockSpec((tm, d), lambda i, f: (i, 0)),
            pl.BlockSpec((1, d), lambda i, f: (0, 0)),
            pl.BlockSpec((d, tf), lambda i, f: (0, f)),
            pl.BlockSpec((tf, d), lambda i, f: (f, 0)),
            pl.BlockSpec((1, d), lambda i, f: (0, 0)),
        ],
        out_specs=pl.BlockSpec((tm, d), lambda i, f: (i, 0)),
        out_shape=jax.ShapeDtypeStruct((m, d), F32),
        scratch_shapes=[pltpu.VMEM((tm, d), BF16), pltpu.VMEM((tm, d), F32)],
        compiler_params=pltpu.CompilerParams(
            dimension_semantics=("parallel", "arbitrary"),
            vmem_limit_bytes=VMEM_LIMIT),
        name="mlp",
    )(h, g_pre, w_up, w_down, g_post)


def _ple_kernel(h_ref, p_ref, g_ref, wg_ref, wp_ref, o_ref):
    h = h_ref[...]
    e = _dot(p_ref[...].astype(BF16), wp_ref[...])
    gate = _sigmoid(_dot(_rms(h, g_ref[...]).astype(BF16), wg_ref[...]))
    o_ref[...] = h + gate * e


def _ple(h, p2, gain, w_gate, w_proj, *, tm):
    m, d = h.shape
    return pl.pallas_call(
        _ple_kernel,
        grid=(m // tm,),
        in_specs=[
            pl.BlockSpec((tm, d), lambda i: (i, 0)),
            pl.BlockSpec((tm, p2.shape[1]), lambda i: (i, 0)),
            _resident(gain.shape),
            _resident(w_gate.shape),
            _resident(w_proj.shape),
        ],
        out_specs=pl.BlockSpec((tm, d), lambda i: (i, 0)),
        out_shape=jax.ShapeDtypeStruct((m, d), F32),
        compiler_params=pltpu.CompilerParams(
            dimension_semantics=("parallel",),
            vmem_limit_bytes=VMEM_LIMIT),
        name="ple",
    )(h, p2, gain, w_gate, w_proj)


def kernel(x, p, norm_mix_pre, norm_mix_post, w_in, w_gate_up, b_gate, gla_norm, w_branch_gla,
           w_branch_sb, w_out, norm_mlp_pre, norm_mlp_post, w_mlp_up, w_mlp_down, norm_ple,
           w_ple_gate, w_ple_proj):
    batch, seq, d = x.shape
    depth = w_in.shape[0]
    m = batch * seq
    sb_w = d // 2
    sb_heads = sb_w // SB_DH
    h = x.reshape(m, d)
    row = lambda g: g.reshape(1, -1)
    for i in range(depth):
        lr0 = 2 * _KEY_W + _VAL_W
        w_t = jnp.swapaxes(w_in[i], 0, 1)
        w_a, w_lr = _head_cast(w_t, n_a=lr0, rank=GATE_RANK, block=256)
        w_gu = jnp.pad(w_gate_up[i], ((0, LANE - GATE_RANK), (0, 0))).astype(BF16)
        z_a, z_lr, u, w_b = _in_proj_a(h, row(norm_mix_pre[i]), w_a, w_lr, w_t,
                                       row_lo=lr0 + GATE_RANK, tm=512)
        z_b = _in_proj_b(u, w_b, tm=2048, tn=1024)
        o_a = _gla(z_a, z_b, z_lr, w_gu, row(b_gate[i]), row(gla_norm[i]),
                   batch=batch, seq=seq, rows=512)
        sq_off = _VAL_W // sb_w
        later_weights = (w_branch_gla[i], w_branch_sb[i], w_out[i], w_mlp_up[i], w_mlp_down[i],
                         w_ple_gate[i])
        o_b, (w_bg, w_bs, w_o, w_up, w_down, w_pg) = _sb_attn(
            z_b, later_weights, batch=batch, seq=seq, heads=sb_heads, q_off=sq_off,
            k_off=sq_off + 1, v_off=sq_off + 2, tq=256, tk=256)
        gate_blk = (_VAL_W + 3 * sb_w) // d
        h = _mix_out(o_a, o_b, z_b, h, w_bg, w_bs, w_o, row(norm_mix_post[i]),
                     ga_blk=gate_blk, gb_blk=gate_blk + 1, tm=512)
        h = _mlp(h, row(norm_mlp_pre[i]), w_up, w_down, row(norm_mlp_post[i]), tm=512, tf=1024)
        h = _ple(h, p[i].reshape(m, -1), row(norm_ple[i]), w_pg, w_ple_proj[i].astype(BF16), tm=512)
    return h.reshape(batch, seq, d)
```

```python
import functools

import jax
import jax.numpy as jnp
from jax import lax
from jax.experimental import pallas as pl
from jax.experimental.pallas import tpu as pltpu

F32 = jnp.float32
BF16 = jnp.bfloat16

EPS = 1e-6
GLA_HEADS = 4
GLA_DK = 128
GLA_DV = 256
GATE_RANK = 16
GATE_TAU = 16.0
GLA_CHUNK = 64
SB_DH = 128
LANE = 128
BF16_SUBLANES = 16
ROW_SPLIT = 2

_KEY_W = GLA_HEADS * GLA_DK
_VAL_W = GLA_HEADS * GLA_DV

VMEM_LIMIT = 56 * 1024 * 1024


def _rms(x, gain):
    ms = jnp.mean(x * x, axis=-1, keepdims=True)
    return x * lax.rsqrt(ms + EPS) * gain


def _sigmoid(x):
    return 1.0 / (1.0 + jnp.exp(-x))


def _split_bf16(x):
    hi = x.astype(BF16)
    lo = (x - hi.astype(F32)).astype(BF16)
    return hi, lo


def _dot(a, b):
    return jnp.dot(a, b, preferred_element_type=F32)


def _dot_nt(a, b):
    return lax.dot_general(a, b, (((1,), (1,)), ((), ())), preferred_element_type=F32)


def _dot_tn(a, b):
    return lax.dot_general(a, b, (((0,), (0,)), ((), ())), preferred_element_type=F32)


def _head_cast_kernel(wt_ref, wa_ref, wlr_ref, *, n_blocks, rank):
    r = pl.program_id(0)
    w = wt_ref[...].T

    @pl.when(r < n_blocks)
    def _():
        wa_ref[...] = w.astype(wa_ref.dtype)

    @pl.when(r == n_blocks)
    def _():
        lr = w[:, :LANE]
        col = lax.broadcasted_iota(jnp.int32, lr.shape, 1)
        wlr_ref[...] = jnp.where(col < rank, lr, 0.0).astype(wlr_ref.dtype)


def _head_cast(w_t, *, n_a, rank, block):
    d = w_t.shape[1]
    n_blocks = n_a // block
    return pl.pallas_call(
        functools.partial(_head_cast_kernel, n_blocks=n_blocks, rank=rank),
        grid=(n_blocks + 1,),
        in_specs=[pl.BlockSpec((block, d), lambda r: (r, 0))],
        out_specs=[pl.BlockSpec((d, block), lambda r: (0, jnp.minimum(r, n_blocks - 1))),
                   pl.BlockSpec((d, LANE), lambda r: (0, 0))],
        out_shape=[jax.ShapeDtypeStruct((d, n_a), BF16), jax.ShapeDtypeStruct((d, LANE), BF16)],
        compiler_params=pltpu.CompilerParams(dimension_semantics=("arbitrary",)),
        name="w_in_head_cast",
    )(w_t)


def _in_proj_a_kernel(x_ref, g_ref, wa_ref, wlr_ref, wrows_ref, za_ref, zlr_ref, u_ref, wb_ref):
    tm = x_ref.shape[0]
    for s in range(ROW_SPLIT):
        rs = slice(s * tm // ROW_SPLIT, (s + 1) * tm // ROW_SPLIT)
        u = _rms(x_ref[rs, :], g_ref[...]).astype(BF16)
        u_ref[rs, :] = u
        zlr_ref[rs, :] = _dot(u, wlr_ref[...]).astype(zlr_ref.dtype)
        za_ref[rs, :] = _dot(u, wa_ref[...]).astype(za_ref.dtype)
    wb_ref[...] = wrows_ref[...].T.astype(wb_ref.dtype)


def _in_proj_a(x2, gain, w_a, w_lr, w_t, *, row_lo, tm):
    m, d = x2.shape
    steps = m // tm
    n_a = w_a.shape[1]
    n_b = w_t.shape[0] - row_lo
    slab = n_b // steps
    assert slab * steps == n_b and slab % LANE == 0 and row_lo % BF16_SUBLANES == 0
    return pl.pallas_call(
        _in_proj_a_kernel,
        grid=(steps,),
        in_specs=[
            pl.BlockSpec((tm, d), lambda i: (i, 0)),
            _resident(gain.shape),
            _resident(w_a.shape),
            _resident(w_lr.shape),
            pl.BlockSpec((pl.Element(slab), pl.Element(d)),
                         lambda i: (pl.multiple_of(row_lo + i * slab, BF16_SUBLANES), 0)),
        ],
        out_specs=[
            pl.BlockSpec((tm, n_a), lambda i: (i, 0)),
            pl.BlockSpec((tm, LANE), lambda i: (i, 0)),
            pl.BlockSpec((tm, d), lambda i: (i, 0)),
            pl.BlockSpec((d, slab), lambda i: (0, i)),
        ],
        out_shape=[
            jax.ShapeDtypeStruct((m, n_a), BF16),
            jax.ShapeDtypeStruct((m, LANE), BF16),
            jax.ShapeDtypeStruct((m, d), BF16),
            jax.ShapeDtypeStruct((d, n_b), BF16),
        ],
        compiler_params=pltpu.CompilerParams(
            dimension_semantics=("parallel",),
            vmem_limit_bytes=VMEM_LIMIT),
        name="in_proj_a",
    )(x2, gain, w_a, w_lr, w_t)


def _in_proj_b_kernel(u_ref, w_ref, z_ref):
    z_ref[...] = _dot(u_ref[...], w_ref[...]).astype(z_ref.dtype)


def _in_proj_b(u, w_b, *, tm, tn):
    m, d = u.shape
    n = w_b.shape[1]
    return pl.pallas_call(
        _in_proj_b_kernel,
        grid=(m // tm, n // tn),
        in_specs=[
            pl.BlockSpec((tm, d), lambda i, j: (i, 0)),
            pl.BlockSpec((d, tn), lambda i, j: (0, j)),
        ],
        out_specs=pl.BlockSpec((tm, tn), lambda i, j: (i, j)),
        out_shape=jax.ShapeDtypeStruct((m, n), BF16),
        compiler_params=pltpu.CompilerParams(
            dimension_semantics=("parallel", "arbitrary"),
            vmem_limit_bytes=VMEM_LIMIT),
        name="in_proj_b",
    )(u, w_b)


def _gla_kernel(q_ref, k_ref, v_ref, go_ref, lr_ref, wgu_ref, bg_ref, gn_ref,
                o_ref, s_ref, *, rows):
    @pl.when(pl.program_id(1) == 0)
    def _():
        s_ref[...] = jnp.zeros_like(s_ref)

    c = GLA_CHUNK
    nc = rows // c
    r_idx = lax.broadcasted_iota(jnp.int32, (c, c), 0)
    c_idx = lax.broadcasted_iota(jnp.int32, (c, c), 1)
    causal = r_idx >= c_idx
    tril = causal.astype(BF16)
    tril2 = jnp.concatenate([tril, tril], axis=1)
    scale = GLA_DK ** -0.5
    gn = gn_ref[...]
    ksl = lambda h: slice(h * GLA_DK, (h + 1) * GLA_DK)
    vsl = lambda h: slice(h * GLA_DV, (h + 1) * GLA_DV)

    pre = _dot(lr_ref[...], wgu_ref[...]) + bg_ref[...]
    la = (jnp.minimum(pre, 0.0) - jnp.log(1.0 + jnp.exp(-jnp.abs(pre)))) * (1.0 / GATE_TAU)
    la_hi, la_lo = _split_bf16(la)

    q_dec, k_intra, k_state, decay = [], [], [], []
    for ci in range(nc):
        rs = slice(ci * c, (ci + 1) * c)
        b = _dot(tril2, jnp.concatenate([la_hi[rs], la_lo[rs]], axis=0))
        b_last = b[c - 1:c, :]
        q = q_ref[rs, :].astype(F32)
        k = k_ref[rs, :].astype(F32)
        q_dec.append((q * scale * jnp.exp(b)).astype(BF16))
        k_intra.append((k * jnp.exp(-b)).astype(BF16))
        k_state.append((k * jnp.exp(b_last - b)).astype(BF16))
        decay.append(jnp.exp(b_last))

    def intra(ci):
        rs = slice(ci * c, (ci + 1) * c)
        out = []
        for h in range(GLA_HEADS):
            v_h = v_ref[rs, vsl(h)]
            scores = _dot_nt(q_dec[ci][:, ksl(h)], k_intra[ci][:, ksl(h)])
            scores = jnp.where(causal, scores, 0.0).astype(BF16)
            out.append((_dot(scores, v_h), _dot_tn(v_h, k_state[ci][:, ksl(h)])))
        return out

    def inter(ci, intra_ci):
        rs = slice(ci * c, (ci + 1) * c)
        for h in range(GLA_HEADS):
            o_intra, kv = intra_ci[h]
            st = s_ref[h]
            o = o_intra + _dot_nt(q_dec[ci][:, ksl(h)], st.astype(BF16))
            s_ref[h] = decay[ci][:, ksl(h)] * st + kv
            o = _rms(o, gn)
            go = go_ref[rs, vsl(h)].astype(F32)
            o_ref[rs, vsl(h)] = (o * (go * _sigmoid(go))).astype(o_ref.dtype)

    ahead = intra(0)
    for ci in range(nc):
        current, ahead = ahead, (intra(ci + 1) if ci + 1 < nc else None)
        inter(ci, current)


def _gla(z_a, z_b, z_lr, w_gu, b_gate, gla_norm, *, batch, seq, rows):
    m = z_a.shape[0]
    nb = seq // rows
    kw, vw = _KEY_W, _VAL_W
    row = lambda b, r: b * nb + r
    return pl.pallas_call(
        functools.partial(_gla_kernel, rows=rows),
        grid=(batch, nb),
        in_specs=[
            pl.BlockSpec((rows, kw), lambda b, r: (row(b, r), 0)),
            pl.BlockSpec((rows, kw), lambda b, r: (row(b, r), 1)),
            pl.BlockSpec((rows, vw), lambda b, r: (row(b, r), 1)),
            pl.BlockSpec((rows, vw), lambda b, r: (row(b, r), 0)),
            pl.BlockSpec((rows, LANE), lambda b, r: (row(b, r), 0)),
            pl.BlockSpec((LANE, kw), lambda b, r: (0, 0)),
            pl.BlockSpec((1, kw), lambda b, r: (0, 0)),
            pl.BlockSpec((1, GLA_DV), lambda b, r: (0, 0)),
        ],
        out_specs=pl.BlockSpec((rows, vw), lambda b, r: (row(b, r), 0)),
        out_shape=jax.ShapeDtypeStruct((m, vw), BF16),
        scratch_shapes=[pltpu.VMEM((GLA_HEADS, GLA_DV, GLA_DK), F32)],
        compiler_params=pltpu.CompilerParams(
            dimension_semantics=("parallel", "arbitrary"),
            vmem_limit_bytes=VMEM_LIMIT),
        name="gla",
    )(z_a, z_a, z_a, z_b, z_lr, w_gu, b_gate, gla_norm)


SB_SKIP_LOG2 = 151.0
LOG2_E = 1.4426950408889634


def _sb_kernel(q_ref, k_ref, v_ref, *rest, tq, tk, heads, n_cast):
    w32_refs, o_ref, w16_refs = rest[:n_cast], rest[n_cast], rest[n_cast + 1:2 * n_cast + 1]
    acc_ref, carry_ref = rest[2 * n_cast + 1:]
    for w32, w16 in zip(w32_refs, w16_refs):
        w16[...] = w32[...].astype(w16.dtype)

    q0 = pl.program_id(1) * tq
    to_log2 = (SB_DH ** -0.5) * LOG2_E
    j_idx = lax.broadcasted_iota(jnp.int32, (tk, tk), 0)
    s_idx = lax.broadcasted_iota(jnp.int32, (tk, tk), 1)
    suffix_ones = (j_idx >= s_idx).astype(BF16)
    suffix_ones2 = jnp.concatenate([suffix_ones, suffix_ones], axis=0)
    t_loc = lax.broadcasted_iota(jnp.int32, (tq, tk), 0)
    s_loc = lax.broadcasted_iota(jnp.int32, (tq, tk), 1)
    hs = lambda h: slice(h * SB_DH, (h + 1) * SB_DH)

    def block(k0, mask, first):
        y, cs, carries = {}, {}, {}

        def scores(h):
            y[h] = _dot_nt(q_ref[:, hs(h)], k_ref[pl.ds(k0, tk), hs(h)]) * to_log2

        def suffix_sums(h):
            sp = jnp.maximum(y[h], 0.0) + jnp.log2(1.0 + jnp.exp2(-jnp.abs(y[h])))
            if mask is not None:
                sp = jnp.where(mask, sp, 0.0)
            hi, lo = _split_bf16(sp)
            cs[h] = _dot(jnp.concatenate([hi, lo], axis=1), suffix_ones2)

        def weighted_values(h):
            a = jnp.exp2(y.pop(h) - cs[h])
            if mask is not None:
                a = jnp.where(mask, a, 0.0)
            pv = _dot(a.astype(BF16), v_ref[pl.ds(k0, tk), hs(h)])
            row_sum = cs.pop(h)[:, 0:1]
            if first:
                acc_ref[:, hs(h)] = pv
                carries[h] = row_sum
            else:
                carry = carry_ref[h]
                acc_ref[:, hs(h)] += pv * jnp.exp2(-carry)
                carries[h] = carry + row_sum
            carry_ref[h] = carries[h]

        stages = (scores, suffix_sums, weighted_values)
        for step in range(heads + len(stages) - 1):
            for lag, stage in enumerate(stages):
                if 0 <= step - lag < heads:
                    stage(step - lag)
        return jnp.min(functools.reduce(jnp.minimum, [carries[h] for h in range(heads)]))

    n_diag = tq // tk
    cmin = None
    for d in range(n_diag):
        k0 = pl.multiple_of(q0 + (n_diag - 1 - d) * tk, tk)
        cmin = block(k0, (s_loc + k0) < (t_loc + q0), d == 0)

    def cond(state):
        kb, cmin = state
        return jnp.logical_and(kb >= 0, cmin < SB_SKIP_LOG2)

    def body(state):
        kb, _ = state
        return kb - 1, block(pl.multiple_of(kb * tk, tk), None, False)

    lax.while_loop(cond, body, (q0 // tk - 1, cmin))
    o_ref[...] = acc_ref[...].astype(o_ref.dtype)


def _sb_attn(z, weights, *, batch, seq, heads, q_off, k_off, v_off, tq, tk):
    m = z.shape[0]
    nq = seq // tq
    w = heads * SB_DH
    steps = batch * nq
    slab = lambda a: pl.BlockSpec((a.shape[0] // steps, a.shape[1]), lambda b, i: (b * nq + i, 0))
    for a in weights:
        assert a.shape[0] % (steps * BF16_SUBLANES) == 0, a.shape
    out = pl.pallas_call(
        functools.partial(_sb_kernel, tq=tq, tk=tk, heads=heads, n_cast=len(weights)),
        grid=(batch, nq),
        in_specs=[
            pl.BlockSpec((tq, w), lambda b, i: (b * nq + i, q_off)),
            pl.BlockSpec((seq, w), lambda b, i: (b, k_off)),
            pl.BlockSpec((seq, w), lambda b, i: (b, v_off)),
        ] + [slab(a) for a in weights],
        out_specs=[pl.BlockSpec((tq, w), lambda b, i: (b * nq + i, 0))] + [slab(a) for a in weights],
        out_shape=[jax.ShapeDtypeStruct((m, w), BF16)]
        + [jax.ShapeDtypeStruct(a.shape, BF16) for a in weights],
        scratch_shapes=[pltpu.VMEM((tq, w), F32), pltpu.VMEM((heads, tq, 1), F32)],
        compiler_params=pltpu.CompilerParams(
            dimension_semantics=("parallel", "arbitrary"),
            vmem_limit_bytes=VMEM_LIMIT),
        name="sb_attn",
    )(z, z, z, *weights)
    return out[0], out[1:]


def _mix_out_kernel(oa_ref, ob_ref, ga_ref, gb_ref, x_ref, wa_ref, wb_ref, wo_ref, g_ref, h_ref):
    tm = h_ref.shape[0]
    halves = [slice(s * tm // ROW_SPLIT, (s + 1) * tm // ROW_SPLIT) for s in range(ROW_SPLIT)]
    branches = [(_dot(oa_ref[rs, :], wa_ref[...]), _dot(ob_ref[rs, :], wb_ref[...])) for rs in halves]
    mixes = []
    for rs, (ya, yb) in zip(halves, branches):
        y = (_sigmoid(ga_ref[rs, :].astype(F32)) * ya + _sigmoid(gb_ref[rs, :].astype(F32)) * yb)
        mixes.append(_dot(y.astype(BF16), wo_ref[...]))
    for rs, mix in zip(halves, mixes):
        h_ref[rs, :] = x_ref[rs, :] + _rms(mix, g_ref[...])


def _resident(shape):
    return pl.BlockSpec(shape, lambda *_: (0,) * len(shape), pipeline_mode=pl.Buffered(1))


def _mix_out(o_a, o_b, z, x2, w_a, w_b, w_o, gain, *, ga_blk, gb_blk, tm):
    m, d = x2.shape
    return pl.pallas_call(
        _mix_out_kernel,
        grid=(m // tm,),
        in_specs=[
            pl.BlockSpec((tm, o_a.shape[1]), lambda i: (i, 0)),
            pl.BlockSpec((tm, o_b.shape[1]), lambda i: (i, 0)),
            pl.BlockSpec((tm, d), lambda i: (i, ga_blk)),
            pl.BlockSpec((tm, d), lambda i: (i, gb_blk)),
            pl.BlockSpec((tm, d), lambda i: (i, 0)),
            _resident(w_a.shape),
            _resident(w_b.shape),
            _resident(w_o.shape),
            _resident(gain.shape),
        ],
        out_specs=pl.BlockSpec((tm, d), lambda i: (i, 0)),
        out_shape=jax.ShapeDtypeStruct((m, d), F32),
        compiler_params=pltpu.CompilerParams(
            dimension_semantics=("parallel",),
            vmem_limit_bytes=VMEM_LIMIT),
        name="mix_out",
    )(o_a, o_b, z, z, x2, w_a, w_b, w_o, gain)


def _mlp_kernel(h_ref, gpre_ref, wu_ref, wd_ref, gpost_ref, o_ref, u_ref):
    f = pl.program_id(1)
    last = pl.num_programs(1) - 1

    def ffn(u):
        a = jnp.maximum(_dot(u, wu_ref[...]), 0.0)
        return _dot((a * a).astype(BF16), wd_ref[...])

    @pl.when(f == 0)
    def _():
        u = _rms(h_ref[...], gpre_ref[...]).astype(BF16)
        u_ref[...] = u
        o_ref[...] = ffn(u)

    @pl.when(jnp.logical_and(f > 0, f < last))
    def _():
        o_ref[...] += ffn(u_ref[...])

    @pl.when(f == last)
    def _():
        total = o_ref[...] + ffn(u_ref[...])
        o_ref[...] = h_ref[...] + _rms(total, gpost_ref[...])


def _mlp(h, g_pre, w_up, w_down, g_post, *, tm, tf):
    m, d = h.shape
    ff = w_up.shape[1]
    assert ff // tf >= 2, "the first and last hidden tiles take different branches"
    return pl.pallas_call(
        _mlp_kernel,
        grid=(m // tm, ff // tf),
        in_specs=[
            pl.BlockSpec((tm, d), lambda i, f: (i, 0)),
            pl.BlockSpec((1, d), lambda i, f: (0, 0)),
            pl.BlockSpec((d, tf), lambda i, f: (0, f)),
            pl.BlockSpec((tf, d), lambda i, f: (f, 0)),
            pl.BlockSpec((1, d), lambda i, f: (0, 0)),
        ],
        out_specs=pl.BlockSpec((tm, d), lambda i, f: (i, 0)),
        out_shape=jax.ShapeDtypeStruct((m, d), F32),
        scratch_shapes=[pltpu.VMEM((tm, d), BF16)],
        compiler_params=pltpu.CompilerParams(
            dimension_semantics=("parallel", "arbitrary"),
            vmem_limit_bytes=VMEM_LIMIT),
        name="mlp",
    )(h, g_pre, w_up, w_down, g_post)


def _ple_kernel(h_ref, p_ref, g_ref, wg_ref, wp_ref, o_ref):
    tm = h_ref.shape[0]
    halves = [slice(s * tm // ROW_SPLIT, (s + 1) * tm // ROW_SPLIT) for s in range(ROW_SPLIT)]
    parts = []
    for rs in halves:
        gate_pre = _dot(_rms(h_ref[rs, :], g_ref[...]).astype(BF16), wg_ref[...])
        parts.append((gate_pre, _dot(p_ref[rs, :].astype(BF16), wp_ref[...])))
    for rs, (gate_pre, e) in zip(halves, parts):
        o_ref[rs, :] = h_ref[rs, :] + _sigmoid(gate_pre) * e


def _ple(h, p2, gain, w_gate, w_proj, *, tm):
    m, d = h.shape
    return pl.pallas_call(
        _ple_kernel,
        grid=(m // tm,),
        in_specs=[
            pl.BlockSpec((tm, d), lambda i: (i, 0)),
            pl.BlockSpec((tm, p2.shape[1]), lambda i: (i, 0)),
            _resident(gain.shape),
            _resident(w_gate.shape),
            _resident(w_proj.shape),
        ],
        out_specs=pl.BlockSpec((tm, d), lambda i: (i, 0)),
        out_shape=jax.ShapeDtypeStruct((m, d), F32),
        compiler_params=pltpu.CompilerParams(
            dimension_semantics=("parallel",),
            vmem_limit_bytes=VMEM_LIMIT),
        name="ple",
    )(h, p2, gain, w_gate, w_proj)


def kernel(x, p, norm_mix_pre, norm_mix_post, w_in, w_gate_up, b_gate, gla_norm, w_branch_gla,
           w_branch_sb, w_out, norm_mlp_pre, norm_mlp_post, w_mlp_up, w_mlp_down, norm_ple,
           w_ple_gate, w_ple_proj):
    batch, seq, d = x.shape
    depth = w_in.shape[0]
    m = batch * seq
    sb_w = d // 2
    sb_heads = sb_w // SB_DH
    h = x.reshape(m, d)
    row = lambda g: g.reshape(1, -1)
    for i in range(depth):
        lr0 = 2 * _KEY_W + _VAL_W
        w_t = jnp.swapaxes(w_in[i], 0, 1)
        w_a, w_lr = _head_cast(w_t, n_a=lr0, rank=GATE_RANK, block=256)
        w_gu = jnp.pad(w_gate_up[i], ((0, LANE - GATE_RANK), (0, 0))).astype(BF16)
        z_a, z_lr, u, w_b = _in_proj_a(h, row(norm_mix_pre[i]), w_a, w_lr, w_t,
                                       row_lo=lr0 + GATE_RANK, tm=512)
        z_b = _in_proj_b(u, w_b, tm=2048, tn=1024)
        o_a = _gla(z_a, z_b, z_lr, w_gu, row(b_gate[i]), row(gla_norm[i]),
                   batch=batch, seq=seq, rows=512)
        sq_off = _VAL_W // sb_w
        later_weights = (w_branch_gla[i], w_branch_sb[i], w_out[i], w_mlp_up[i], w_mlp_down[i],
                         w_ple_gate[i])
        o_b, (w_bg, w_bs, w_o, w_up, w_down, w_pg) = _sb_attn(
            z_b, later_weights, batch=batch, seq=seq, heads=sb_heads, q_off=sq_off,
            k_off=sq_off + 1, v_off=sq_off + 2, tq=256, tk=256)
        gate_blk = (_VAL_W + 3 * sb_w) // d
        h = _mix_out(o_a, o_b, z_b, h, w_bg, w_bs, w_o, row(norm_mix_post[i]),
                     ga_blk=gate_blk, gb_blk=gate_blk + 1, tm=512)
        h = _mlp(h, row(norm_mlp_pre[i]), w_up, w_down, row(norm_mlp_post[i]), tm=1024, tf=512)
        h = _ple(h, p[i].reshape(m, -1), row(norm_ple[i]), w_pg, w_ple_proj[i].astype(BF16), tm=512)
    return h.reshape(batch, seq, d)
```

```python
import functools

import jax
import jax.numpy as jnp
from jax import lax
from jax.experimental import pallas as pl
from jax.experimental.pallas import tpu as pltpu

F32 = jnp.float32
BF16 = jnp.bfloat16

EPS = 1e-6
GLA_HEADS = 4
GLA_DK = 128
GLA_DV = 256
GATE_RANK = 16
GATE_TAU = 16.0
GLA_CHUNK = 64
SB_DH = 128
LANE = 128
BF16_SUBLANES = 16
ROW_SPLIT = 2

_KEY_W = GLA_HEADS * GLA_DK
_VAL_W = GLA_HEADS * GLA_DV

VMEM_LIMIT = 56 * 1024 * 1024


def _rms(x, gain):
    ms = jnp.mean(x * x, axis=-1, keepdims=True)
    return x * lax.rsqrt(ms + EPS) * gain


def _sigmoid(x):
    return 1.0 / (1.0 + jnp.exp(-x))


def _split_bf16(x):
    hi = x.astype(BF16)
    lo = (x - hi.astype(F32)).astype(BF16)
    return hi, lo


def _dot(a, b):
    return jnp.dot(a, b, preferred_element_type=F32)


def _dot_nt(a, b):
    return lax.dot_general(a, b, (((1,), (1,)), ((), ())), preferred_element_type=F32)


def _dot_tn(a, b):
    return lax.dot_general(a, b, (((0,), (0,)), ((), ())), preferred_element_type=F32)


def _head_cast_kernel(wt_ref, wa_ref, wlr_ref, *, n_blocks, rank):
    r = pl.program_id(0)
    w = wt_ref[...].T

    @pl.when(r < n_blocks)
    def _():
        wa_ref[...] = w.astype(wa_ref.dtype)

    @pl.when(r == n_blocks)
    def _():
        lr = w[:, :LANE]
        col = lax.broadcasted_iota(jnp.int32, lr.shape, 1)
        wlr_ref[...] = jnp.where(col < rank, lr, 0.0).astype(wlr_ref.dtype)


def _head_cast(w_t, *, n_a, rank, block):
    d = w_t.shape[1]
    n_blocks = n_a // block
    return pl.pallas_call(
        functools.partial(_head_cast_kernel, n_blocks=n_blocks, rank=rank),
        grid=(n_blocks + 1,),
        in_specs=[pl.BlockSpec((block, d), lambda r: (r, 0))],
        out_specs=[pl.BlockSpec((d, block), lambda r: (0, jnp.minimum(r, n_blocks - 1))),
                   pl.BlockSpec((d, LANE), lambda r: (0, 0))],
        out_shape=[jax.ShapeDtypeStruct((d, n_a), BF16), jax.ShapeDtypeStruct((d, LANE), BF16)],
        compiler_params=pltpu.CompilerParams(dimension_semantics=("arbitrary",)),
        name="w_in_head_cast",
    )(w_t)


def _in_proj_a_kernel(x_ref, g_ref, wa_ref, wlr_ref, wrows_ref, za_ref, zlr_ref, u_ref, wb_ref):
    tm = x_ref.shape[0]
    for s in range(ROW_SPLIT):
        rs = slice(s * tm // ROW_SPLIT, (s + 1) * tm // ROW_SPLIT)
        u = _rms(x_ref[rs, :], g_ref[...]).astype(BF16)
        u_ref[rs, :] = u
        zlr_ref[rs, :] = _dot(u, wlr_ref[...]).astype(zlr_ref.dtype)
        za_ref[rs, :] = _dot(u, wa_ref[...]).astype(za_ref.dtype)
    wb_ref[...] = wrows_ref[...].T.astype(wb_ref.dtype)


def _in_proj_a(x2, gain, w_a, w_lr, w_t, *, row_lo, tm):
    m, d = x2.shape
    steps = m // tm
    n_a = w_a.shape[1]
    n_b = w_t.shape[0] - row_lo
    slab = n_b // steps
    assert slab * steps == n_b and slab % LANE == 0 and row_lo % BF16_SUBLANES == 0
    return pl.pallas_call(
        _in_proj_a_kernel,
        grid=(steps,),
        in_specs=[
            pl.BlockSpec((tm, d), lambda i: (i, 0)),
            _resident(gain.shape),
            _resident(w_a.shape),
            _resident(w_lr.shape),
            pl.BlockSpec((pl.Element(slab), pl.Element(d)),
                         lambda i: (pl.multiple_of(row_lo + i * slab, BF16_SUBLANES), 0)),
        ],
        out_specs=[
            pl.BlockSpec((tm, n_a), lambda i: (i, 0)),
            pl.BlockSpec((tm, LANE), lambda i: (i, 0)),
            pl.BlockSpec((tm, d), lambda i: (i, 0)),
            pl.BlockSpec((d, slab), lambda i: (0, i)),
        ],
        out_shape=[
            jax.ShapeDtypeStruct((m, n_a), BF16),
            jax.ShapeDtypeStruct((m, LANE), BF16),
            jax.ShapeDtypeStruct((m, d), BF16),
            jax.ShapeDtypeStruct((d, n_b), BF16),
        ],
        compiler_params=pltpu.CompilerParams(
            dimension_semantics=("parallel",),
            vmem_limit_bytes=VMEM_LIMIT),
        name="in_proj_a",
    )(x2, gain, w_a, w_lr, w_t)


def _in_proj_b_kernel(u_ref, w_ref, z_ref):
    z_ref[...] = _dot(u_ref[...], w_ref[...]).astype(z_ref.dtype)


def _in_proj_b(u, w_b, *, tm, tn):
    m, d = u.shape
    n = w_b.shape[1]
    return pl.pallas_call(
        _in_proj_b_kernel,
        grid=(m // tm, n // tn),
        in_specs=[
            pl.BlockSpec((tm, d), lambda i, j: (i, 0)),
            pl.BlockSpec((d, tn), lambda i, j: (0, j)),
        ],
        out_specs=pl.BlockSpec((tm, tn), lambda i, j: (i, j)),
        out_shape=jax.ShapeDtypeStruct((m, n), BF16),
        compiler_params=pltpu.CompilerParams(
            dimension_semantics=("parallel", "arbitrary"),
            vmem_limit_bytes=VMEM_LIMIT),
        name="in_proj_b",
    )(u, w_b)


def _gla_kernel(q_ref, k_ref, v_ref, go_ref, lr_ref, wgu_ref, bg_ref, gn_ref,
                o_ref, s_ref, *, rows):
    @pl.when(pl.program_id(1) == 0)
    def _():
        s_ref[...] = jnp.zeros_like(s_ref)

    c = GLA_CHUNK
    nc = rows // c
    r_idx = lax.broadcasted_iota(jnp.int32, (c, c), 0)
    c_idx = lax.broadcasted_iota(jnp.int32, (c, c), 1)
    causal = r_idx >= c_idx
    tril = causal.astype(BF16)
    tril2 = jnp.concatenate([tril, tril], axis=1)
    scale = GLA_DK ** -0.5
    gn = gn_ref[...]
    ksl = lambda h: slice(h * GLA_DK, (h + 1) * GLA_DK)
    vsl = lambda h: slice(h * GLA_DV, (h + 1) * GLA_DV)

    pre = _dot(lr_ref[...], wgu_ref[...]) + bg_ref[...]
    la = (jnp.minimum(pre, 0.0) - jnp.log(1.0 + jnp.exp(-jnp.abs(pre)))) * (1.0 / GATE_TAU)
    la_hi, la_lo = _split_bf16(la)

    q_dec, k_intra, k_state, decay = [], [], [], []
    for ci in range(nc):
        rs = slice(ci * c, (ci + 1) * c)
        b = _dot(tril2, jnp.concatenate([la_hi[rs], la_lo[rs]], axis=0))
        b_last = b[c - 1:c, :]
        q = q_ref[rs, :].astype(F32)
        k = k_ref[rs, :].astype(F32)
        q_dec.append((q * scale * jnp.exp(b)).astype(BF16))
        k_intra.append((k * jnp.exp(-b)).astype(BF16))
        k_state.append((k * jnp.exp(b_last - b)).astype(BF16))
        decay.append(jnp.exp(b_last))

    def intra(ci):
        rs = slice(ci * c, (ci + 1) * c)
        out = []
        for h in range(GLA_HEADS):
            v_h = v_ref[rs, vsl(h)]
            scores = _dot_nt(q_dec[ci][:, ksl(h)], k_intra[ci][:, ksl(h)])
            scores = jnp.where(causal, scores, 0.0).astype(BF16)
            out.append((_dot(scores, v_h), _dot_tn(v_h, k_state[ci][:, ksl(h)])))
        return out

    def inter(ci, intra_ci):
        rs = slice(ci * c, (ci + 1) * c)
        for h in range(GLA_HEADS):
            o_intra, kv = intra_ci[h]
            st = s_ref[h]
            o = o_intra + _dot_nt(q_dec[ci][:, ksl(h)], st.astype(BF16))
            s_ref[h] = decay[ci][:, ksl(h)] * st + kv
            o = _rms(o, gn)
            go = go_ref[rs, vsl(h)].astype(F32)
            o_ref[rs, vsl(h)] = (o * (go * _sigmoid(go))).astype(o_ref.dtype)

    ahead = intra(0)
    for ci in range(nc):
        current, ahead = ahead, (intra(ci + 1) if ci + 1 < nc else None)
        inter(ci, current)


def _gla(z_a, z_b, z_lr, w_gu, b_gate, gla_norm, *, batch, seq, rows):
    m = z_a.shape[0]
    nb = seq // rows
    kw, vw = _KEY_W, _VAL_W
    row = lambda b, r: b * nb + r
    return pl.pallas_call(
        functools.partial(_gla_kernel, rows=rows),
        grid=(batch, nb),
        in_specs=[
            pl.BlockSpec((rows, kw), lambda b, r: (row(b, r), 0)),
            pl.BlockSpec((rows, kw), lambda b, r: (row(b, r), 1)),
            pl.BlockSpec((rows, vw), lambda b, r: (row(b, r), 1)),
            pl.BlockSpec((rows, vw), lambda b, r: (row(b, r), 0)),
            pl.BlockSpec((rows, LANE), lambda b, r: (row(b, r), 0)),
            pl.BlockSpec((LANE, kw), lambda b, r: (0, 0)),
            pl.BlockSpec((1, kw), lambda b, r: (0, 0)),
            pl.BlockSpec((1, GLA_DV), lambda b, r: (0, 0)),
        ],
        out_specs=pl.BlockSpec((rows, vw), lambda b, r: (row(b, r), 0)),
        out_shape=jax.ShapeDtypeStruct((m, vw), BF16),
        scratch_shapes=[pltpu.VMEM((GLA_HEADS, GLA_DV, GLA_DK), F32)],
        compiler_params=pltpu.CompilerParams(
            dimension_semantics=("parallel", "arbitrary"),
            vmem_limit_bytes=VMEM_LIMIT),
        name="gla",
    )(z_a, z_a, z_a, z_b, z_lr, w_gu, b_gate, gla_norm)


SB_SKIP_LOG2 = 151.0
LOG2_E = 1.4426950408889634


def _sb_kernel(q_ref, k_ref, v_ref, *rest, tq, tk, heads, n_cast):
    w32_refs, o_ref, w16_refs = rest[:n_cast], rest[n_cast], rest[n_cast + 1:2 * n_cast + 1]
    acc_ref, carry_ref = rest[2 * n_cast + 1:]
    for w32, w16 in zip(w32_refs, w16_refs):
        w16[...] = w32[...].astype(w16.dtype)

    q0 = pl.program_id(1) * tq
    to_log2 = (SB_DH ** -0.5) * LOG2_E
    j_idx = lax.broadcasted_iota(jnp.int32, (tk, tk), 0)
    s_idx = lax.broadcasted_iota(jnp.int32, (tk, tk), 1)
    suffix_ones = (j_idx >= s_idx).astype(BF16)
    suffix_ones2 = jnp.concatenate([suffix_ones, suffix_ones], axis=0)
    t_loc = lax.broadcasted_iota(jnp.int32, (tq, tk), 0)
    s_loc = lax.broadcasted_iota(jnp.int32, (tq, tk), 1)
    hs = lambda h: slice(h * SB_DH, (h + 1) * SB_DH)

    def blocks(specs):
        tiles = [(k0, mask, first, h) for (k0, mask, first) in specs for h in range(heads)]
        y, cs, carries = {}, {}, {}

        def scores(t):
            k0, _, _, h = tiles[t]
            y[t] = _dot_nt(q_ref[:, hs(h)], k_ref[pl.ds(k0, tk), hs(h)]) * to_log2

        def suffix_sums(t):
            mask = tiles[t][1]
            sp = jnp.maximum(y[t], 0.0) + jnp.log2(1.0 + jnp.exp2(-jnp.abs(y[t])))
            if mask is not None:
                sp = jnp.where(mask, sp, 0.0)
            hi, lo = _split_bf16(sp)
            cs[t] = _dot(jnp.concatenate([hi, lo], axis=1), suffix_ones2)

        def weighted_values(t):
            k0, mask, first, h = tiles[t]
            a = jnp.exp2(y.pop(t) - cs[t])
            if mask is not None:
                a = jnp.where(mask, a, 0.0)
            pv = _dot(a.astype(BF16), v_ref[pl.ds(k0, tk), hs(h)])
            row_sum = cs.pop(t)[:, 0:1]
            if first:
                acc_ref[:, hs(h)] = pv
                carries[h] = row_sum
            else:
                carry = carries[h] if h in carries else carry_ref[h]
                acc_ref[:, hs(h)] += pv * jnp.exp2(-carry)
                carries[h] = carry + row_sum
            carry_ref[h] = carries[h]

        stages = (scores, suffix_sums, weighted_values)
        for step in range(len(tiles) + len(stages) - 1):
            for lag, stage in enumerate(stages):
                if 0 <= step - lag < len(tiles):
                    stage(step - lag)
        return jnp.min(functools.reduce(jnp.minimum, [carries[h] for h in range(heads)]))

    def finish(kb, cmin):
        def cond(state):
            kb, cmin = state
            return jnp.logical_and(kb >= 0, cmin < SB_SKIP_LOG2)

        def body(state):
            kb, _ = state
            return kb - 1, blocks([(pl.multiple_of(kb * tk, tk), None, False)])

        lax.while_loop(cond, body, (kb, cmin))
        o_ref[...] = acc_ref[...].astype(o_ref.dtype)

    n_diag = tq // tk
    diag = []
    for d in range(n_diag):
        k0 = pl.multiple_of(q0 + (n_diag - 1 - d) * tk, tk)
        diag.append((k0, (s_loc + k0) < (t_loc + q0), d == 0))
    kb_next = q0 // tk - 1

    @pl.when(kb_next < 0)
    def _():
        finish(kb_next, blocks(diag))

    @pl.when(kb_next >= 0)
    def _():
        below = (pl.multiple_of(kb_next * tk, tk), None, False)
        finish(kb_next - 1, blocks(diag + [below]))


def _sb_attn(z, weights, *, batch, seq, heads, q_off, k_off, v_off, tq, tk):
    m = z.shape[0]
    nq = seq // tq
    w = heads * SB_DH
    steps = batch * nq
    slab = lambda a: pl.BlockSpec((a.shape[0] // steps, a.shape[1]), lambda b, i: (b * nq + i, 0))
    for a in weights:
        assert a.shape[0] % (steps * BF16_SUBLANES) == 0, a.shape
    out = pl.pallas_call(
        functools.partial(_sb_kernel, tq=tq, tk=tk, heads=heads, n_cast=len(weights)),
        grid=(batch, nq),
        in_specs=[
            pl.BlockSpec((tq, w), lambda b, i: (b * nq + i, q_off)),
            pl.BlockSpec((seq, w), lambda b, i: (b, k_off)),
            pl.BlockSpec((seq, w), lambda b, i: (b, v_off)),
        ] + [slab(a) for a in weights],
        out_specs=[pl.BlockSpec((tq, w), lambda b, i: (b * nq + i, 0))] + [slab(a) for a in weights],
        out_shape=[jax.ShapeDtypeStruct((m, w), BF16)]
        + [jax.ShapeDtypeStruct(a.shape, BF16) for a in weights],
        scratch_shapes=[pltpu.VMEM((tq, w), F32), pltpu.VMEM((heads, tq, 1), F32)],
        compiler_params=pltpu.CompilerParams(
            dimension_semantics=("parallel", "arbitrary"),
            vmem_limit_bytes=VMEM_LIMIT),
        name="sb_attn",
    )(z, z, z, *weights)
    return out[0], out[1:]


def _mix_out_kernel(oa_ref, ob_ref, ga_ref, gb_ref, x_ref, wa_ref, wb_ref, wo_ref, g_ref, h_ref):
    tm = h_ref.shape[0]
    halves = [slice(s * tm // ROW_SPLIT, (s + 1) * tm // ROW_SPLIT) for s in range(ROW_SPLIT)]
    branches = [(_dot(oa_ref[rs, :], wa_ref[...]), _dot(ob_ref[rs, :], wb_ref[...])) for rs in halves]
    mixes = []
    for rs, (ya, yb) in zip(halves, branches):
        y = (_sigmoid(ga_ref[rs, :].astype(F32)) * ya + _sigmoid(gb_ref[rs, :].astype(F32)) * yb)
        mixes.append(_dot(y.astype(BF16), wo_ref[...]))
    for rs, mix in zip(halves, mixes):
        h_ref[rs, :] = x_ref[rs, :] + _rms(mix, g_ref[...])


def _resident(shape):
    return pl.BlockSpec(shape, lambda *_: (0,) * len(shape), pipeline_mode=pl.Buffered(1))


def _mix_out(o_a, o_b, z, x2, w_a, w_b, w_o, gain, *, ga_blk, gb_blk, tm):
    m, d = x2.shape
    return pl.pallas_call(
        _mix_out_kernel,
        grid=(m // tm,),
        in_specs=[
            pl.BlockSpec((tm, o_a.shape[1]), lambda i: (i, 0)),
            pl.BlockSpec((tm, o_b.shape[1]), lambda i: (i, 0)),
            pl.BlockSpec((tm, d), lambda i: (i, ga_blk)),
            pl.BlockSpec((tm, d), lambda i: (i, gb_blk)),
            pl.BlockSpec((tm, d), lambda i: (i, 0)),
            _resident(w_a.shape),
            _resident(w_b.shape),
            _resident(w_o.shape),
            _resident(gain.shape),
        ],
        out_specs=pl.BlockSpec((tm, d), lambda i: (i, 0)),
        out_shape=jax.ShapeDtypeStruct((m, d), F32),
        compiler_params=pltpu.CompilerParams(
            dimension_semantics=("parallel",),
            vmem_limit_bytes=VMEM_LIMIT),
        name="mix_out",
    )(o_a, o_b, z, z, x2, w_a, w_b, w_o, gain)


def _mlp_kernel(h_ref, gpre_ref, wu_ref, wd_ref, gpost_ref, o_ref, u_ref, acc_ref):
    f = pl.program_id(1)
    last = pl.num_programs(1) - 1

    def ffn(u):
        a = jnp.maximum(_dot(u, wu_ref[...]), 0.0)
        return _dot((a * a).astype(BF16), wd_ref[...])

    @pl.when(f == 0)
    def _():
        u = _rms(h_ref[...], gpre_ref[...]).astype(BF16)
        u_ref[...] = u
        acc_ref[...] = ffn(u)

    @pl.when(jnp.logical_and(f > 0, f < last))
    def _():
        acc_ref[...] += ffn(u_ref[...])

    @pl.when(f == last)
    def _():
        total = acc_ref[...] + ffn(u_ref[...])
        o_ref[...] = h_ref[...] + _rms(total, gpost_ref[...])


def _mlp(h, g_pre, w_up, w_down, g_post, *, tm, tf):
    m, d = h.shape
    ff = w_up.shape[1]
    assert ff // tf >= 2, "the first and last hidden tiles take different branches"
    return pl.pallas_call(
        _mlp_kernel,
        grid=(m // tm, ff // tf),
        in_specs=[
            pl.BlockSpec((tm, d), lambda i, f: (i, 0)),
            pl.BlockSpec((1, d), lambda i, f: (0, 0)),
            pl.BlockSpec((d, tf), lambda i, f: (0, f)),
            pl.BlockSpec((tf, d), lambda i, f: (f, 0)),
            pl.BlockSpec((1, d), lambda i, f: (0, 0)),
        ],
        out_specs=pl.BlockSpec((tm, d), lambda i, f: (i, 0)),
        out_shape=jax.ShapeDtypeStruct((m, d), F32),
        scratch_shapes=[pltpu.VMEM((tm, d), BF16), pltpu.VMEM((tm, d), F32)],
        compiler_params=pltpu.CompilerParams(
            dimension_semantics=("parallel", "arbitrary"),
            vmem_limit_bytes=VMEM_LIMIT),
        name="mlp",
    )(h, g_pre, w_up, w_down, g_post)


def _ple_kernel(h_ref, p_ref, g_ref, wg_ref, wp_ref, o_ref):
    tm = h_ref.shape[0]
    halves = [slice(s * tm // ROW_SPLIT, (s + 1) * tm // ROW_SPLIT) for s in range(ROW_SPLIT)]
    parts = []
    for rs in halves:
        gate_pre = _dot(_rms(h_ref[rs, :], g_ref[...]).astype(BF16), wg_ref[...])
        parts.append((gate_pre, _dot(p_ref[rs, :].astype(BF16), wp_ref[...])))
    for rs, (gate_pre, e) in zip(halves, parts):
        o_ref[rs, :] = h_ref[rs, :] + _sigmoid(gate_pre) * e


def _ple(h, p2, gain, w_gate, w_proj, *, tm):
    m, d = h.shape
    return pl.pallas_call(
        _ple_kernel,
        grid=(m // tm,),
        in_specs=[
            pl.BlockSpec((tm, d), lambda i: (i, 0)),
            pl.BlockSpec((tm, p2.shape[1]), lambda i: (i, 0)),
            _resident(gain.shape),
            _resident(w_gate.shape),
            _resident(w_proj.shape),
        ],
        out_specs=pl.BlockSpec((tm, d), lambda i: (i, 0)),
        out_shape=jax.ShapeDtypeStruct((m, d), F32),
        compiler_params=pltpu.CompilerParams(
            dimension_semantics=("parallel",),
            vmem_limit_bytes=VMEM_LIMIT),
        name="ple",
    )(h, p2, gain, w_gate, w_proj)


def kernel(x, p, norm_mix_pre, norm_mix_post, w_in, w_gate_up, b_gate, gla_norm, w_branch_gla,
           w_branch_sb, w_out, norm_mlp_pre, norm_mlp_post, w_mlp_up, w_mlp_down, norm_ple,
           w_ple_gate, w_ple_proj):
    batch, seq, d = x.shape
    depth = w_in.shape[0]
    m = batch * seq
    sb_w = d // 2
    sb_heads = sb_w // SB_DH
    h = x.reshape(m, d)
    row = lambda g: g.reshape(1, -1)
    for i in range(depth):
        lr0 = 2 * _KEY_W + _VAL_W
        w_t = jnp.swapaxes(w_in[i], 0, 1)
        w_a, w_lr = _head_cast(w_t, n_a=lr0, rank=GATE_RANK, block=256)
        w_gu = jnp.pad(w_gate_up[i], ((0, LANE - GATE_RANK), (0, 0))).astype(BF16)
        z_a, z_lr, u, w_b = _in_proj_a(h, row(norm_mix_pre[i]), w_a, w_lr, w_t,
                                       row_lo=lr0 + GATE_RANK, tm=512)
        z_b = _in_proj_b(u, w_b, tm=2048, tn=1024)
        o_a = _gla(z_a, z_b, z_lr, w_gu, row(b_gate[i]), row(gla_norm[i]),
                   batch=batch, seq=seq, rows=512)
        sq_off = _VAL_W // sb_w
        later_weights = (w_branch_gla[i], w_branch_sb[i], w_out[i], w_mlp_up[i], w_mlp_down[i],
                         w_ple_gate[i])
        o_b, (w_bg, w_bs, w_o, w_up, w_down, w_pg) = _sb_attn(
            z_b, later_weights, batch=batch, seq=seq, heads=sb_heads, q_off=sq_off,
            k_off=sq_off + 1, v_off=sq_off + 2, tq=256, tk=256)
        gate_blk = (_VAL_W + 3 * sb_w) // d
        h = _mix_out(o_a, o_b, z_b, h, w_bg, w_bs, w_o, row(norm_mix_post[i]),
                     ga_blk=gate_blk, gb_blk=gate_blk + 1, tm=512)
        h = _mlp(h, row(norm_mlp_pre[i]), w_up, w_down, row(norm_mlp_post[i]), tm=512, tf=1024)
        h = _ple(h, p[i].reshape(m, -1), row(norm_ple[i]), w_pg, w_ple_proj[i].astype(BF16), tm=512)
    return h.reshape(batch, seq, d)
```

```python
import functools

import jax
import jax.numpy as jnp
from jax import lax
from jax.experimental import pallas as pl
from jax.experimental.pallas import tpu as pltpu

F32 = jnp.float32
BF16 = jnp.bfloat16

EPS = 1e-6
GLA_HEADS = 4
GLA_DK = 128
GLA_DV = 256
GATE_RANK = 16
GATE_TAU = 16.0
GLA_CHUNK = 64
SB_DH = 128
LANE = 128
BF16_SUBLANES = 16
ROW_SPLIT = 2

_KEY_W = GLA_HEADS * GLA_DK
_VAL_W = GLA_HEADS * GLA_DV

VMEM_LIMIT = 56 * 1024 * 1024

TILES = dict(
    head_cast_block=256,
    in_proj_a_tm=512,
    in_proj_b_tm=2048, in_proj_b_tn=1024,
    gla_rows=1024,
    sb_tq=256, sb_tk=256,
    mix_out_tm=512,
    mlp_tm=512, mlp_tf=1024,
    ple_tm=512,
)


def _rms(x, gain):
    ms = jnp.mean(x * x, axis=-1, keepdims=True)
    return x * lax.rsqrt(ms + EPS) * gain


def _sigmoid(x):
    return 1.0 / (1.0 + jnp.exp(-x))


def _split_bf16(x):
    hi = x.astype(BF16)
    lo = (x - hi.astype(F32)).astype(BF16)
    return hi, lo


def _dot(a, b):
    return jnp.dot(a, b, preferred_element_type=F32)


def _dot_nt(a, b):
    return lax.dot_general(a, b, (((1,), (1,)), ((), ())), preferred_element_type=F32)


def _dot_tn(a, b):
    return lax.dot_general(a, b, (((0,), (0,)), ((), ())), preferred_element_type=F32)


def _resident(shape):
    return pl.BlockSpec(shape, lambda *_: (0,) * len(shape), pipeline_mode=pl.Buffered(1))


def _row_groups(tm):
    return [slice(s * tm // ROW_SPLIT, (s + 1) * tm // ROW_SPLIT) for s in range(ROW_SPLIT)]


def _head_cast_kernel(wt_ref, wa_ref, wlr_ref, *, n_blocks, rank):
    r = pl.program_id(0)
    w = wt_ref[...].T

    @pl.when(r < n_blocks)
    def _():
        wa_ref[...] = w.astype(wa_ref.dtype)

    @pl.when(r == n_blocks)
    def _():
        lr = w[:, :LANE]
        col = lax.broadcasted_iota(jnp.int32, lr.shape, 1)
        wlr_ref[...] = jnp.where(col < rank, lr, 0.0).astype(wlr_ref.dtype)


def _head_cast(w_t, *, n_a, rank, block):
    d = w_t.shape[1]
    n_blocks = n_a // block
    return pl.pallas_call(
        functools.partial(_head_cast_kernel, n_blocks=n_blocks, rank=rank),
        grid=(n_blocks + 1,),
        in_specs=[pl.BlockSpec((block, d), lambda r: (r, 0))],
        out_specs=[pl.BlockSpec((d, block), lambda r: (0, jnp.minimum(r, n_blocks - 1))),
                   pl.BlockSpec((d, LANE), lambda r: (0, 0))],
        out_shape=[jax.ShapeDtypeStruct((d, n_a), BF16), jax.ShapeDtypeStruct((d, LANE), BF16)],
        compiler_params=pltpu.CompilerParams(dimension_semantics=("arbitrary",)),
        name="w_in_head_cast",
    )(w_t)


def _in_proj_a_kernel(x_ref, g_ref, wa_ref, wlr_ref, wrows_ref, za_ref, zlr_ref, u_ref, wb_ref):
    for rs in _row_groups(x_ref.shape[0]):
        u = _rms(x_ref[rs, :], g_ref[...]).astype(BF16)
        u_ref[rs, :] = u
        zlr_ref[rs, :] = _dot(u, wlr_ref[...]).astype(zlr_ref.dtype)
        za_ref[rs, :] = _dot(u, wa_ref[...]).astype(za_ref.dtype)
    wb_ref[...] = wrows_ref[...].T.astype(wb_ref.dtype)


def _in_proj_a(x2, gain, w_a, w_lr, w_t, *, row_lo, tm):
    m, d = x2.shape
    steps = m // tm
    n_a = w_a.shape[1]
    n_b = w_t.shape[0] - row_lo
    slab = n_b // steps
    assert slab * steps == n_b and slab % LANE == 0 and row_lo % BF16_SUBLANES == 0
    return pl.pallas_call(
        _in_proj_a_kernel,
        grid=(steps,),
        in_specs=[
            pl.BlockSpec((tm, d), lambda i: (i, 0)),
            _resident(gain.shape),
            _resident(w_a.shape),
            _resident(w_lr.shape),
            pl.BlockSpec((pl.Element(slab), pl.Element(d)),
                         lambda i: (pl.multiple_of(row_lo + i * slab, BF16_SUBLANES), 0)),
        ],
        out_specs=[
            pl.BlockSpec((tm, n_a), lambda i: (i, 0)),
            pl.BlockSpec((tm, LANE), lambda i: (i, 0)),
            pl.BlockSpec((tm, d), lambda i: (i, 0)),
            pl.BlockSpec((d, slab), lambda i: (0, i)),
        ],
        out_shape=[
            jax.ShapeDtypeStruct((m, n_a), BF16),
            jax.ShapeDtypeStruct((m, LANE), BF16),
            jax.ShapeDtypeStruct((m, d), BF16),
            jax.ShapeDtypeStruct((d, n_b), BF16),
        ],
        compiler_params=pltpu.CompilerParams(
            dimension_semantics=("parallel",),
            vmem_limit_bytes=VMEM_LIMIT),
        name="in_proj_a",
    )(x2, gain, w_a, w_lr, w_t)


def _in_proj_b_kernel(u_ref, w_ref, z_ref):
    z_ref[...] = _dot(u_ref[...], w_ref[...]).astype(z_ref.dtype)


def _in_proj_b(u, w_b, *, tm, tn):
    m, d = u.shape
    n = w_b.shape[1]
    return pl.pallas_call(
        _in_proj_b_kernel,
        grid=(m // tm, n // tn),
        in_specs=[
            pl.BlockSpec((tm, d), lambda i, j: (i, 0)),
            pl.BlockSpec((d, tn), lambda i, j: (0, j)),
        ],
        out_specs=pl.BlockSpec((tm, tn), lambda i, j: (i, j)),
        out_shape=jax.ShapeDtypeStruct((m, n), BF16),
        compiler_params=pltpu.CompilerParams(
            dimension_semantics=("parallel", "arbitrary"),
            vmem_limit_bytes=VMEM_LIMIT),
        name="in_proj_b",
    )(u, w_b)


def _gla_kernel(q_ref, k_ref, v_ref, go_ref, lr_ref, wgu_ref, bg_ref, gn_ref,
                o_ref, s_ref, *, rows):
    @pl.when(pl.program_id(1) == 0)
    def _():
        s_ref[...] = jnp.zeros_like(s_ref)

    c = GLA_CHUNK
    nc = rows // c
    r_idx = lax.broadcasted_iota(jnp.int32, (c, c), 0)
    c_idx = lax.broadcasted_iota(jnp.int32, (c, c), 1)
    causal = r_idx >= c_idx
    tril = causal.astype(BF16)
    tril2 = jnp.concatenate([tril, tril], axis=1)
    scale = GLA_DK ** -0.5
    gn = gn_ref[...]
    ksl = lambda h: slice(h * GLA_DK, (h + 1) * GLA_DK)
    vsl = lambda h: slice(h * GLA_DV, (h + 1) * GLA_DV)

    pre = _dot(lr_ref[...], wgu_ref[...]) + bg_ref[...]
    la = (jnp.minimum(pre, 0.0) - jnp.log(1.0 + jnp.exp(-jnp.abs(pre)))) * (1.0 / GATE_TAU)
    la_hi, la_lo = _split_bf16(la)

    q_dec, k_intra, k_state, decay = [], [], [], []
    for ci in range(nc):
        rs = slice(ci * c, (ci + 1) * c)
        b = _dot(tril2, jnp.concatenate([la_hi[rs], la_lo[rs]], axis=0))
        b_last = b[c - 1:c, :]
        q = q_ref[rs, :].astype(F32)
        k = k_ref[rs, :].astype(F32)
        q_dec.append((q * scale * jnp.exp(b)).astype(BF16))
        k_intra.append((k * jnp.exp(-b)).astype(BF16))
        k_state.append((k * jnp.exp(b_last - b)).astype(BF16))
        decay.append(jnp.exp(b_last))

    def intra(ci):
        rs = slice(ci * c, (ci + 1) * c)
        out = []
        for h in range(GLA_HEADS):
            v_h = v_ref[rs, vsl(h)]
            scores = _dot_nt(q_dec[ci][:, ksl(h)], k_intra[ci][:, ksl(h)])
            scores = jnp.where(causal, scores, 0.0).astype(BF16)
            out.append((_dot(scores, v_h), _dot_tn(v_h, k_state[ci][:, ksl(h)])))
        return out

    def inter(ci, intra_ci):
        rs = slice(ci * c, (ci + 1) * c)
        for h in range(GLA_HEADS):
            o_intra, kv = intra_ci[h]
            st = s_ref[h]
            o = o_intra + _dot_nt(q_dec[ci][:, ksl(h)], st.astype(BF16))
            s_ref[h] = decay[ci][:, ksl(h)] * st + kv
            o = _rms(o, gn)
            go = go_ref[rs, vsl(h)].astype(F32)
            o_ref[rs, vsl(h)] = (o * (go * _sigmoid(go))).astype(o_ref.dtype)

    ahead = intra(0)
    for ci in range(nc):
        current, ahead = ahead, (intra(ci + 1) if ci + 1 < nc else None)
        inter(ci, current)


def _gla(z_a, z_b, z_lr, w_gu, b_gate, gla_norm, *, batch, seq, rows):
    m = z_a.shape[0]
    nb = seq // rows
    kw, vw = _KEY_W, _VAL_W
    row = lambda b, r: b * nb + r
    return pl.pallas_call(
        functools.partial(_gla_kernel, rows=rows),
        grid=(batch, nb),
        in_specs=[
            pl.BlockSpec((rows, kw), lambda b, r: (row(b, r), 0)),
            pl.BlockSpec((rows, kw), lambda b, r: (row(b, r), 1)),
            pl.BlockSpec((rows, vw), lambda b, r: (row(b, r), 1)),
            pl.BlockSpec((rows, vw), lambda b, r: (row(b, r), 0)),
            pl.BlockSpec((rows, LANE), lambda b, r: (row(b, r), 0)),
            pl.BlockSpec((LANE, kw), lambda b, r: (0, 0)),
            pl.BlockSpec((1, kw), lambda b, r: (0, 0)),
            pl.BlockSpec((1, GLA_DV), lambda b, r: (0, 0)),
        ],
        out_specs=pl.BlockSpec((rows, vw), lambda b, r: (row(b, r), 0)),
        out_shape=jax.ShapeDtypeStruct((m, vw), BF16),
        scratch_shapes=[pltpu.VMEM((GLA_HEADS, GLA_DV, GLA_DK), F32)],
        compiler_params=pltpu.CompilerParams(
            dimension_semantics=("parallel", "arbitrary"),
            vmem_limit_bytes=VMEM_LIMIT),
        name="gla",
    )(z_a, z_a, z_a, z_b, z_lr, w_gu, b_gate, gla_norm)


SB_SKIP_LOG2 = 151.0
LOG2_E = 1.4426950408889634


def _sb_kernel(q_ref, k_ref, v_ref, *rest, tq, tk, heads, n_cast):
    w32_refs, o_ref, w16_refs = rest[:n_cast], rest[n_cast], rest[n_cast + 1:2 * n_cast + 1]
    acc_ref, carry_ref = rest[2 * n_cast + 1:]
    for w32, w16 in zip(w32_refs, w16_refs):
        w16[...] = w32[...].astype(w16.dtype)

    q0 = pl.program_id(1) * tq
    to_log2 = (SB_DH ** -0.5) * LOG2_E
    j_idx = lax.broadcasted_iota(jnp.int32, (tk, tk), 0)
    s_idx = lax.broadcasted_iota(jnp.int32, (tk, tk), 1)
    suffix_ones = (j_idx >= s_idx).astype(BF16)
    suffix_ones2 = jnp.concatenate([suffix_ones, suffix_ones], axis=0)
    t_loc = lax.broadcasted_iota(jnp.int32, (tq, tk), 0)
    s_loc = lax.broadcasted_iota(jnp.int32, (tq, tk), 1)
    hs = lambda h: slice(h * SB_DH, (h + 1) * SB_DH)

    def blocks(specs):
        tiles = [(k0, mask, first, h) for (k0, mask, first) in specs for h in range(heads)]
        y, cs, carries = {}, {}, {}

        def scores(t):
            k0, _, _, h = tiles[t]
            y[t] = _dot_nt(q_ref[:, hs(h)], k_ref[pl.ds(k0, tk), hs(h)]) * to_log2

        def suffix_sums(t):
            mask = tiles[t][1]
            sp = jnp.maximum(y[t], 0.0) + jnp.log2(1.0 + jnp.exp2(-jnp.abs(y[t])))
            if mask is not None:
                sp = jnp.where(mask, sp, 0.0)
            hi, lo = _split_bf16(sp)
            cs[t] = _dot(jnp.concatenate([hi, lo], axis=1), suffix_ones2)

        def weighted_values(t):
            k0, mask, first, h = tiles[t]
            a = jnp.exp2(y.pop(t) - cs[t])
            if mask is not None:
                a = jnp.where(mask, a, 0.0)
            pv = _dot(a.astype(BF16), v_ref[pl.ds(k0, tk), hs(h)])
            row_sum = cs.pop(t)[:, 0:1]
            if first:
                acc_ref[:, hs(h)] = pv
                carries[h] = row_sum
            else:
                carry = carries[h] if h in carries else carry_ref[h]
                acc_ref[:, hs(h)] += pv * jnp.exp2(-carry)
                carries[h] = carry + row_sum
            carry_ref[h] = carries[h]

        stages = (scores, suffix_sums, weighted_values)
        for step in range(len(tiles) + len(stages) - 1):
            for lag, stage in enumerate(stages):
                if 0 <= step - lag < len(tiles):
                    stage(step - lag)
        return jnp.min(functools.reduce(jnp.minimum, [carries[h] for h in range(heads)]))

    def finish(kb, cmin):
        def cond(state):
            kb, cmin = state
            return jnp.logical_and(kb >= 0, cmin < SB_SKIP_LOG2)

        def body(state):
            kb, _ = state
            return kb - 1, blocks([(pl.multiple_of(kb * tk, tk), None, False)])

        lax.while_loop(cond, body, (kb, cmin))
        o_ref[...] = acc_ref[...].astype(o_ref.dtype)

    n_diag = tq // tk
    diag = []
    for d in range(n_diag):
        k0 = pl.multiple_of(q0 + (n_diag - 1 - d) * tk, tk)
        diag.append((k0, (s_loc + k0) < (t_loc + q0), d == 0))
    kb_next = q0 // tk - 1

    @pl.when(kb_next < 0)
    def _():
        finish(kb_next, blocks(diag))

    @pl.when(kb_next >= 0)
    def _():
        below = (pl.multiple_of(kb_next * tk, tk), None, False)
        finish(kb_next - 1, blocks(diag + [below]))


def _sb_attn(z, weights, *, batch, seq, heads, q_off, k_off, v_off, tq, tk):
    m = z.shape[0]
    nq = seq // tq
    w = heads * SB_DH
    steps = batch * nq
    slab = lambda a: pl.BlockSpec((a.shape[0] // steps, a.shape[1]), lambda b, i: (b * nq + i, 0))
    for a in weights:
        assert a.shape[0] % (steps * BF16_SUBLANES) == 0, a.shape
    out = pl.pallas_call(
        functools.partial(_sb_kernel, tq=tq, tk=tk, heads=heads, n_cast=len(weights)),
        grid=(batch, nq),
        in_specs=[
            pl.BlockSpec((tq, w), lambda b, i: (b * nq + i, q_off)),
            pl.BlockSpec((seq, w), lambda b, i: (b, k_off)),
            pl.BlockSpec((seq, w), lambda b, i: (b, v_off)),
        ] + [slab(a) for a in weights],
        out_specs=[pl.BlockSpec((tq, w), lambda b, i: (b * nq + i, 0))] + [slab(a) for a in weights],
        out_shape=[jax.ShapeDtypeStruct((m, w), BF16)]
        + [jax.ShapeDtypeStruct(a.shape, BF16) for a in weights],
        scratch_shapes=[pltpu.VMEM((tq, w), F32), pltpu.VMEM((heads, tq, 1), F32)],
        compiler_params=pltpu.CompilerParams(
            dimension_semantics=("parallel", "arbitrary"),
            vmem_limit_bytes=VMEM_LIMIT),
        name="sb_attn",
    )(z, z, z, *weights)
    return out[0], out[1:]


def _mix_out_kernel(oa_ref, ob_ref, ga_ref, gb_ref, x_ref, wa_ref, wb_ref, wo_ref, g_ref, h_ref):
    groups = _row_groups(h_ref.shape[0])
    branches = [(_dot(oa_ref[rs, :], wa_ref[...]), _dot(ob_ref[rs, :], wb_ref[...])) for rs in groups]
    mixes = []
    for rs, (ya, yb) in zip(groups, branches):
        y = (_sigmoid(ga_ref[rs, :].astype(F32)) * ya + _sigmoid(gb_ref[rs, :].astype(F32)) * yb)
        mixes.append(_dot(y.astype(BF16), wo_ref[...]))
    for rs, mix in zip(groups, mixes):
        h_ref[rs, :] = x_ref[rs, :] + _rms(mix, g_ref[...])


def _mix_out(o_a, o_b, z, x2, w_a, w_b, w_o, gain, *, ga_blk, gb_blk, tm):
    m, d = x2.shape
    return pl.pallas_call(
        _mix_out_kernel,
        grid=(m // tm,),
        in_specs=[
            pl.BlockSpec((tm, o_a.shape[1]), lambda i: (i, 0)),
            pl.BlockSpec((tm, o_b.shape[1]), lambda i: (i, 0)),
            pl.BlockSpec((tm, d), lambda i: (i, ga_blk)),
            pl.BlockSpec((tm, d), lambda i: (i, gb_blk)),
            pl.BlockSpec((tm, d), lambda i: (i, 0)),
            _resident(w_a.shape),
            _resident(w_b.shape),
            _resident(w_o.shape),
            _resident(gain.shape),
        ],
        out_specs=pl.BlockSpec((tm, d), lambda i: (i, 0)),
        out_shape=jax.ShapeDtypeStruct((m, d), F32),
        compiler_params=pltpu.CompilerParams(
            dimension_semantics=("parallel",),
            vmem_limit_bytes=VMEM_LIMIT),
        name="mix_out",
    )(o_a, o_b, z, z, x2, w_a, w_b, w_o, gain)


def _mlp_kernel(h_ref, gpre_ref, wu_ref, wd_ref, gpost_ref, o_ref, u_ref, acc_ref):
    f = pl.program_id(1)
    last = pl.num_programs(1) - 1

    def ffn(u):
        a = jnp.maximum(_dot(u, wu_ref[...]), 0.0)
        return _dot((a * a).astype(BF16), wd_ref[...])

    @pl.when(f == 0)
    def _():
        u = _rms(h_ref[...], gpre_ref[...]).astype(BF16)
        u_ref[...] = u
        acc_ref[...] = ffn(u)

    @pl.when(jnp.logical_and(f > 0, f < last))
    def _():
        acc_ref[...] += ffn(u_ref[...])

    @pl.when(f == last)
    def _():
        total = acc_ref[...] + ffn(u_ref[...])
        o_ref[...] = h_ref[...] + _rms(total, gpost_ref[...])


def _mlp(h, g_pre, w_up, w_down, g_post, *, tm, tf):
    m, d = h.shape
    ff = w_up.shape[1]
    assert ff // tf >= 2, "the first and last hidden tiles take different branches"
    return pl.pallas_call(
        _mlp_kernel,
        grid=(m // tm, ff // tf),
        in_specs=[
            pl.BlockSpec((tm, d), lambda i, f: (i, 0)),
            pl.BlockSpec((1, d), lambda i, f: (0, 0)),
            pl.BlockSpec((d, tf), lambda i, f: (0, f)),
            pl.BlockSpec((tf, d), lambda i, f: (f, 0)),
            pl.BlockSpec((1, d), lambda i, f: (0, 0)),
        ],
        out_specs=pl.BlockSpec((tm, d), lambda i, f: (i, 0)),
        out_shape=jax.ShapeDtypeStruct((m, d), F32),
        scratch_shapes=[pltpu.VMEM((tm, d), BF16), pltpu.VMEM((tm, d), F32)],
        compiler_params=pltpu.CompilerParams(
            dimension_semantics=("parallel", "arbitrary"),
            vmem_limit_bytes=VMEM_LIMIT),
        name="mlp",
    )(h, g_pre, w_up, w_down, g_post)


def _ple_kernel(h_ref, p_ref, g_ref, wg_ref, wp_ref, o_ref):
    groups = _row_groups(h_ref.shape[0])
    parts = []
    for rs in groups:
        gate_pre = _dot(_rms(h_ref[rs, :], g_ref[...]).astype(BF16), wg_ref[...])
        parts.append((gate_pre, _dot(p_ref[rs, :].astype(BF16), wp_ref[...])))
    for rs, (gate_pre, e) in zip(groups, parts):
        o_ref[rs, :] = h_ref[rs, :] + _sigmoid(gate_pre) * e


def _ple(h, p2, gain, w_gate, w_proj, *, tm):
    m, d = h.shape
    return pl.pallas_call(
        _ple_kernel,
        grid=(m // tm,),
        in_specs=[
            pl.BlockSpec((tm, d), lambda i: (i, 0)),
            pl.BlockSpec((tm, p2.shape[1]), lambda i: (i, 0)),
            _resident(gain.shape),
            _resident(w_gate.shape),
            _resident(w_proj.shape),
        ],
        out_specs=pl.BlockSpec((tm, d), lambda i: (i, 0)),
        out_shape=jax.ShapeDtypeStruct((m, d), F32),
        compiler_params=pltpu.CompilerParams(
            dimension_semantics=("parallel",),
            vmem_limit_bytes=VMEM_LIMIT),
        name="ple",
    )(h, p2, gain, w_gate, w_proj)


def kernel(x, p, norm_mix_pre, norm_mix_post, w_in, w_gate_up, b_gate, gla_norm, w_branch_gla,
           w_branch_sb, w_out, norm_mlp_pre, norm_mlp_post, w_mlp_up, w_mlp_down, norm_ple,
           w_ple_gate, w_ple_proj):
    batch, seq, d = x.shape
    depth = w_in.shape[0]
    m = batch * seq
    sb_w = d // 2
    sb_heads = sb_w // SB_DH
    t = TILES
    h = x.reshape(m, d)
    row = lambda g: g.reshape(1, -1)
    for i in range(depth):
        lr0 = 2 * _KEY_W + _VAL_W
        w_t = jnp.swapaxes(w_in[i], 0, 1)
        w_a, w_lr = _head_cast(w_t, n_a=lr0, rank=GATE_RANK, block=t["head_cast_block"])
        w_gu = jnp.pad(w_gate_up[i], ((0, LANE - GATE_RANK), (0, 0))).astype(BF16)
        z_a, z_lr, u, w_b = _in_proj_a(h, row(norm_mix_pre[i]), w_a, w_lr, w_t,
                                       row_lo=lr0 + GATE_RANK, tm=t["in_proj_a_tm"])
        z_b = _in_proj_b(u, w_b, tm=t["in_proj_b_tm"], tn=t["in_proj_b_tn"])
        o_a = _gla(z_a, z_b, z_lr, w_gu, row(b_gate[i]), row(gla_norm[i]),
                   batch=batch, seq=seq, rows=t["gla_rows"])
        sq_off = _VAL_W // sb_w
        later_weights = (w_branch_gla[i], w_branch_sb[i], w_out[i], w_mlp_up[i], w_mlp_down[i],
                         w_ple_gate[i])
        o_b, (w_bg, w_bs, w_o, w_up, w_down, w_pg) = _sb_attn(
            z_b, later_weights, batch=batch, seq=seq, heads=sb_heads, q_off=sq_off,
            k_off=sq_off + 1, v_off=sq_off + 2, tq=t["sb_tq"], tk=t["sb_tk"])
        gate_blk = (_VAL_W + 3 * sb_w) // d
        h = _mix_out(o_a, o_b, z_b, h, w_bg, w_bs, w_o, row(norm_mix_post[i]),
                     ga_blk=gate_blk, gb_blk=gate_blk + 1, tm=t["mix_out_tm"])
        h = _mlp(h, row(norm_mlp_pre[i]), w_up, w_down, row(norm_mlp_post[i]),
                 tm=t["mlp_tm"], tf=t["mlp_tf"])
        h = _ple(h, p[i].reshape(m, -1), row(norm_ple[i]), w_pg, w_ple_proj[i].astype(BF16),
                 tm=t["ple_tm"])
    return h.reshape(batch, seq, d)
```

```python
import functools

import jax
import jax.numpy as jnp
from jax import lax
from jax.experimental import pallas as pl
from jax.experimental.pallas import tpu as pltpu

F32 = jnp.float32
BF16 = jnp.bfloat16

EPS = 1e-6
GLA_HEADS = 4
GLA_DK = 128
GLA_DV = 256
GATE_RANK = 16
GATE_TAU = 16.0
GLA_CHUNK = 64
SB_DH = 128
LANE = 128
BF16_SUBLANES = 16
ROW_SPLIT = 2

_KEY_W = GLA_HEADS * GLA_DK
_VAL_W = GLA_HEADS * GLA_DV

VMEM_LIMIT = 56 * 1024 * 1024

TILES = dict(
    head_cast_block=256,
    in_proj_a_tm=512,
    in_proj_b_tm=2048, in_proj_b_tn=1024,
    gla_rows=2048,
    sb_tq=256, sb_tk=256,
    mix_out_tm=512,
    mlp_tm=512, mlp_tf=1024,
    ple_tm=1024,
)


def _rms(x, gain):
    ms = jnp.mean(x * x, axis=-1, keepdims=True)
    return x * lax.rsqrt(ms + EPS) * gain


def _sigmoid(x):
    return 1.0 / (1.0 + jnp.exp(-x))


def _split_bf16(x):
    hi = x.astype(BF16)
    lo = (x - hi.astype(F32)).astype(BF16)
    return hi, lo


def _dot(a, b):
    return jnp.dot(a, b, preferred_element_type=F32)


def _dot_nt(a, b):
    return lax.dot_general(a, b, (((1,), (1,)), ((), ())), preferred_element_type=F32)


def _dot_tn(a, b):
    return lax.dot_general(a, b, (((0,), (0,)), ((), ())), preferred_element_type=F32)


def _resident(shape):
    return pl.BlockSpec(shape, lambda *_: (0,) * len(shape), pipeline_mode=pl.Buffered(1))


def _row_groups(tm):
    return [slice(s * tm // ROW_SPLIT, (s + 1) * tm // ROW_SPLIT) for s in range(ROW_SPLIT)]


def _head_cast_kernel(wt_ref, wa_ref, wlr_ref, *, n_blocks, rank):
    r = pl.program_id(0)
    w = wt_ref[...].T

    @pl.when(r < n_blocks)
    def _():
        wa_ref[...] = w.astype(wa_ref.dtype)

    @pl.when(r == n_blocks)
    def _():
        lr = w[:, :LANE]
        col = lax.broadcasted_iota(jnp.int32, lr.shape, 1)
        wlr_ref[...] = jnp.where(col < rank, lr, 0.0).astype(wlr_ref.dtype)


def _head_cast(w_t, *, n_a, rank, block):
    d = w_t.shape[1]
    n_blocks = n_a // block
    return pl.pallas_call(
        functools.partial(_head_cast_kernel, n_blocks=n_blocks, rank=rank),
        grid=(n_blocks + 1,),
        in_specs=[pl.BlockSpec((block, d), lambda r: (r, 0))],
        out_specs=[pl.BlockSpec((d, block), lambda r: (0, jnp.minimum(r, n_blocks - 1))),
                   pl.BlockSpec((d, LANE), lambda r: (0, 0))],
        out_shape=[jax.ShapeDtypeStruct((d, n_a), BF16), jax.ShapeDtypeStruct((d, LANE), BF16)],
        compiler_params=pltpu.CompilerParams(dimension_semantics=("arbitrary",)),
        name="w_in_head_cast",
    )(w_t)


def _in_proj_a_kernel(x_ref, g_ref, wa_ref, wlr_ref, wrows_ref, za_ref, zlr_ref, u_ref, wb_ref):
    for rs in _row_groups(x_ref.shape[0]):
        u = _rms(x_ref[rs, :], g_ref[...]).astype(BF16)
        u_ref[rs, :] = u
        zlr_ref[rs, :] = _dot(u, wlr_ref[...]).astype(zlr_ref.dtype)
        za_ref[rs, :] = _dot(u, wa_ref[...]).astype(za_ref.dtype)
    wb_ref[...] = wrows_ref[...].T.astype(wb_ref.dtype)


def _in_proj_a(x2, gain, w_a, w_lr, w_t, *, row_lo, tm):
    m, d = x2.shape
    steps = m // tm
    n_a = w_a.shape[1]
    n_b = w_t.shape[0] - row_lo
    slab = n_b // steps
    assert slab * steps == n_b and slab % LANE == 0 and row_lo % BF16_SUBLANES == 0
    return pl.pallas_call(
        _in_proj_a_kernel,
        grid=(steps,),
        in_specs=[
            pl.BlockSpec((tm, d), lambda i: (i, 0)),
            _resident(gain.shape),
            _resident(w_a.shape),
            _resident(w_lr.shape),
            pl.BlockSpec((pl.Element(slab), pl.Element(d)),
                         lambda i: (pl.multiple_of(row_lo + i * slab, BF16_SUBLANES), 0)),
        ],
        out_specs=[
            pl.BlockSpec((tm, n_a), lambda i: (i, 0)),
            pl.BlockSpec((tm, LANE), lambda i: (i, 0)),
            pl.BlockSpec((tm, d), lambda i: (i, 0)),
            pl.BlockSpec((d, slab), lambda i: (0, i)),
        ],
        out_shape=[
            jax.ShapeDtypeStruct((m, n_a), BF16),
            jax.ShapeDtypeStruct((m, LANE), BF16),
            jax.ShapeDtypeStruct((m, d), BF16),
            jax.ShapeDtypeStruct((d, n_b), BF16),
        ],
        compiler_params=pltpu.CompilerParams(
            dimension_semantics=("parallel",),
            vmem_limit_bytes=VMEM_LIMIT),
        name="in_proj_a",
    )(x2, gain, w_a, w_lr, w_t)


def _in_proj_b_kernel(u_ref, w_ref, z_ref):
    z_ref[...] = _dot(u_ref[...], w_ref[...]).astype(z_ref.dtype)


def _in_proj_b(u, w_b, *, tm, tn):
    m, d = u.shape
    n = w_b.shape[1]
    return pl.pallas_call(
        _in_proj_b_kernel,
        grid=(m // tm, n // tn),
        in_specs=[
            pl.BlockSpec((tm, d), lambda i, j: (i, 0)),
            pl.BlockSpec((d, tn), lambda i, j: (0, j)),
        ],
        out_specs=pl.BlockSpec((tm, tn), lambda i, j: (i, j)),
        out_shape=jax.ShapeDtypeStruct((m, n), BF16),
        compiler_params=pltpu.CompilerParams(
            dimension_semantics=("parallel", "arbitrary"),
            vmem_limit_bytes=VMEM_LIMIT),
        name="in_proj_b",
    )(u, w_b)


def _gla_kernel(q_ref, k_ref, v_ref, go_ref, lr_ref, wgu_ref, bg_ref, gn_ref,
                o_ref, s_ref, *, rows):
    @pl.when(pl.program_id(1) == 0)
    def _():
        s_ref[...] = jnp.zeros_like(s_ref)

    c = GLA_CHUNK
    nc = rows // c
    r_idx = lax.broadcasted_iota(jnp.int32, (c, c), 0)
    c_idx = lax.broadcasted_iota(jnp.int32, (c, c), 1)
    causal = r_idx >= c_idx
    tril = causal.astype(BF16)
    tril2 = jnp.concatenate([tril, tril], axis=1)
    scale = GLA_DK ** -0.5
    gn = gn_ref[...]
    ksl = lambda h: slice(h * GLA_DK, (h + 1) * GLA_DK)
    vsl = lambda h: slice(h * GLA_DV, (h + 1) * GLA_DV)

    pre = _dot(lr_ref[...], wgu_ref[...]) + bg_ref[...]
    la = (jnp.minimum(pre, 0.0) - jnp.log(1.0 + jnp.exp(-jnp.abs(pre)))) * (1.0 / GATE_TAU)
    la_hi, la_lo = _split_bf16(la)

    q_dec, k_intra, k_state, decay = [], [], [], []
    for ci in range(nc):
        rs = slice(ci * c, (ci + 1) * c)
        b = _dot(tril2, jnp.concatenate([la_hi[rs], la_lo[rs]], axis=0))
        b_last = b[c - 1:c, :]
        q = q_ref[rs, :].astype(F32)
        k = k_ref[rs, :].astype(F32)
        q_dec.append((q * scale * jnp.exp(b)).astype(BF16))
        k_intra.append((k * jnp.exp(-b)).astype(BF16))
        k_state.append((k * jnp.exp(b_last - b)).astype(BF16))
        decay.append(jnp.exp(b_last))

    def intra(ci):
        rs = slice(ci * c, (ci + 1) * c)
        out = []
        for h in range(GLA_HEADS):
            v_h = v_ref[rs, vsl(h)]
            scores = _dot_nt(q_dec[ci][:, ksl(h)], k_intra[ci][:, ksl(h)])
            scores = jnp.where(causal, scores, 0.0).astype(BF16)
            out.append((_dot(scores, v_h), _dot_tn(v_h, k_state[ci][:, ksl(h)])))
        return out

    def inter(ci, intra_ci):
        rs = slice(ci * c, (ci + 1) * c)
        for h in range(GLA_HEADS):
            o_intra, kv = intra_ci[h]
            st = s_ref[h]
            o = o_intra + _dot_nt(q_dec[ci][:, ksl(h)], st.astype(BF16))
            s_ref[h] = decay[ci][:, ksl(h)] * st + kv
            o = _rms(o, gn)
            go = go_ref[rs, vsl(h)].astype(F32)
            o_ref[rs, vsl(h)] = (o * (go * _sigmoid(go))).astype(o_ref.dtype)

    ahead = intra(0)
    for ci in range(nc):
        current, ahead = ahead, (intra(ci + 1) if ci + 1 < nc else None)
        inter(ci, current)


def _gla(z_a, z_b, z_lr, w_gu, b_gate, gla_norm, *, batch, seq, rows):
    m = z_a.shape[0]
    nb = seq // rows
    kw, vw = _KEY_W, _VAL_W
    row = lambda b, r: b * nb + r
    return pl.pallas_call(
        functools.partial(_gla_kernel, rows=rows),
        grid=(batch, nb),
        in_specs=[
            pl.BlockSpec((rows, kw), lambda b, r: (row(b, r), 0)),
            pl.BlockSpec((rows, kw), lambda b, r: (row(b, r), 1)),
            pl.BlockSpec((rows, vw), lambda b, r: (row(b, r), 1)),
            pl.BlockSpec((rows, vw), lambda b, r: (row(b, r), 0)),
            pl.BlockSpec((rows, LANE), lambda b, r: (row(b, r), 0)),
            pl.BlockSpec((LANE, kw), lambda b, r: (0, 0)),
            pl.BlockSpec((1, kw), lambda b, r: (0, 0)),
            pl.BlockSpec((1, GLA_DV), lambda b, r: (0, 0)),
        ],
        out_specs=pl.BlockSpec((rows, vw), lambda b, r: (row(b, r), 0)),
        out_shape=jax.ShapeDtypeStruct((m, vw), BF16),
        scratch_shapes=[pltpu.VMEM((GLA_HEADS, GLA_DV, GLA_DK), F32)],
        compiler_params=pltpu.CompilerParams(
            dimension_semantics=("parallel", "arbitrary"),
            vmem_limit_bytes=VMEM_LIMIT),
        name="gla",
    )(z_a, z_a, z_a, z_b, z_lr, w_gu, b_gate, gla_norm)


SB_SKIP_LOG2 = 151.0
LOG2_E = 1.4426950408889634


def _sb_kernel(q_ref, k_ref, v_ref, *rest, tq, tk, heads, n_cast):
    w32_refs, o_ref, w16_refs = rest[:n_cast], rest[n_cast], rest[n_cast + 1:2 * n_cast + 1]
    acc_ref, carry_ref = rest[2 * n_cast + 1:]
    for w32, w16 in zip(w32_refs, w16_refs):
        w16[...] = w32[...].astype(w16.dtype)

    q0 = pl.program_id(1) * tq
    to_log2 = (SB_DH ** -0.5) * LOG2_E
    j_idx = lax.broadcasted_iota(jnp.int32, (tk, tk), 0)
    s_idx = lax.broadcasted_iota(jnp.int32, (tk, tk), 1)
    suffix_ones = (j_idx >= s_idx).astype(BF16)
    suffix_ones2 = jnp.concatenate([suffix_ones, suffix_ones], axis=0)
    t_loc = lax.broadcasted_iota(jnp.int32, (tq, tk), 0)
    s_loc = lax.broadcasted_iota(jnp.int32, (tq, tk), 1)
    hs = lambda h: slice(h * SB_DH, (h + 1) * SB_DH)

    def blocks(specs):
        tiles = [(k0, mask, first, h) for (k0, mask, first) in specs for h in range(heads)]
        y, cs, carries = {}, {}, {}

        def scores(t):
            k0, _, _, h = tiles[t]
            y[t] = _dot_nt(q_ref[:, hs(h)], k_ref[pl.ds(k0, tk), hs(h)]) * to_log2

        def suffix_sums(t):
            mask = tiles[t][1]
            sp = jnp.maximum(y[t], 0.0) + jnp.log2(1.0 + jnp.exp2(-jnp.abs(y[t])))
            if mask is not None:
                sp = jnp.where(mask, sp, 0.0)
            hi, lo = _split_bf16(sp)
            cs[t] = _dot(jnp.concatenate([hi, lo], axis=1), suffix_ones2)

        def weighted_values(t):
            k0, mask, first, h = tiles[t]
            a = jnp.exp2(y.pop(t) - cs[t])
            if mask is not None:
                a = jnp.where(mask, a, 0.0)
            pv = _dot(a.astype(BF16), v_ref[pl.ds(k0, tk), hs(h)])
            row_sum = cs.pop(t)[:, 0:1]
            if first:
                acc_ref[:, hs(h)] = pv
                carries[h] = row_sum
            else:
                carry = carries[h] if h in carries else carry_ref[h]
                acc_ref[:, hs(h)] += pv * jnp.exp2(-carry)
                carries[h] = carry + row_sum
            carry_ref[h] = carries[h]

        stages = (scores, suffix_sums, weighted_values)
        for step in range(len(tiles) + len(stages) - 1):
            for lag, stage in enumerate(stages):
                if 0 <= step - lag < len(tiles):
                    stage(step - lag)
        return jnp.min(functools.reduce(jnp.minimum, [carries[h] for h in range(heads)]))

    def finish(kb, cmin):
        def cond(state):
            kb, cmin = state
            return jnp.logical_and(kb >= 0, cmin < SB_SKIP_LOG2)

        def body(state):
            kb, _ = state
            return kb - 1, blocks([(pl.multiple_of(kb * tk, tk), None, False)])

        lax.while_loop(cond, body, (kb, cmin))
        o_ref[...] = acc_ref[...].astype(o_ref.dtype)

    n_diag = tq // tk
    diag = []
    for d in range(n_diag):
        k0 = pl.multiple_of(q0 + (n_diag - 1 - d) * tk, tk)
        diag.append((k0, (s_loc + k0) < (t_loc + q0), d == 0))
    kb_next = q0 // tk - 1

    @pl.when(kb_next < 0)
    def _():
        finish(kb_next, blocks(diag))

    @pl.when(kb_next >= 0)
    def _():
        below = (pl.multiple_of(kb_next * tk, tk), None, False)
        finish(kb_next - 1, blocks(diag + [below]))


def _sb_attn(z, weights, *, batch, seq, heads, q_off, k_off, v_off, tq, tk):
    m = z.shape[0]
    nq = seq // tq
    w = heads * SB_DH
    steps = batch * nq
    slab = lambda a: pl.BlockSpec((a.shape[0] // steps, a.shape[1]), lambda b, i: (b * nq + i, 0))
    for a in weights:
        assert a.shape[0] % (steps * BF16_SUBLANES) == 0, a.shape
    out = pl.pallas_call(
        functools.partial(_sb_kernel, tq=tq, tk=tk, heads=heads, n_cast=len(weights)),
        grid=(batch, nq),
        in_specs=[
            pl.BlockSpec((tq, w), lambda b, i: (b * nq + i, q_off)),
            pl.BlockSpec((seq, w), lambda b, i: (b, k_off)),
            pl.BlockSpec((seq, w), lambda b, i: (b, v_off)),
        ] + [slab(a) for a in weights],
        out_specs=[pl.BlockSpec((tq, w), lambda b, i: (b * nq + i, 0))] + [slab(a) for a in weights],
        out_shape=[jax.ShapeDtypeStruct((m, w), BF16)]
        + [jax.ShapeDtypeStruct(a.shape, BF16) for a in weights],
        scratch_shapes=[pltpu.VMEM((tq, w), F32), pltpu.VMEM((heads, tq, 1), F32)],
        compiler_params=pltpu.CompilerParams(
            dimension_semantics=("parallel", "arbitrary"),
            vmem_limit_bytes=VMEM_LIMIT),
        name="sb_attn",
    )(z, z, z, *weights)
    return out[0], out[1:]


def _mix_out_kernel(oa_ref, ob_ref, ga_ref, gb_ref, x_ref, wa_ref, wb_ref, wo_ref, g_ref, h_ref):
    groups = _row_groups(h_ref.shape[0])
    branches = [(_dot(oa_ref[rs, :], wa_ref[...]), _dot(ob_ref[rs, :], wb_ref[...])) for rs in groups]
    mixes = []
    for rs, (ya, yb) in zip(groups, branches):
        y = (_sigmoid(ga_ref[rs, :].astype(F32)) * ya + _sigmoid(gb_ref[rs, :].astype(F32)) * yb)
        mixes.append(_dot(y.astype(BF16), wo_ref[...]))
    for rs, mix in zip(groups, mixes):
        h_ref[rs, :] = x_ref[rs, :] + _rms(mix, g_ref[...])


def _mix_out(o_a, o_b, z, x2, w_a, w_b, w_o, gain, *, ga_blk, gb_blk, tm):
    m, d = x2.shape
    return pl.pallas_call(
        _mix_out_kernel,
        grid=(m // tm,),
        in_specs=[
            pl.BlockSpec((tm, o_a.shape[1]), lambda i: (i, 0)),
            pl.BlockSpec((tm, o_b.shape[1]), lambda i: (i, 0)),
            pl.BlockSpec((tm, d), lambda i: (i, ga_blk)),
            pl.BlockSpec((tm, d), lambda i: (i, gb_blk)),
            pl.BlockSpec((tm, d), lambda i: (i, 0)),
            _resident(w_a.shape),
            _resident(w_b.shape),
            _resident(w_o.shape),
            _resident(gain.shape),
        ],
        out_specs=pl.BlockSpec((tm, d), lambda i: (i, 0)),
        out_shape=jax.ShapeDtypeStruct((m, d), F32),
        compiler_params=pltpu.CompilerParams(
            dimension_semantics=("parallel",),
            vmem_limit_bytes=VMEM_LIMIT),
        name="mix_out",
    )(o_a, o_b, z, z, x2, w_a, w_b, w_o, gain)


def _mlp_kernel(h_ref, gpre_ref, wu_ref, wd_ref, gpost_ref, o_ref, u_ref, acc_ref):
    f = pl.program_id(1)
    last = pl.num_programs(1) - 1

    def ffn(u):
        a = jnp.maximum(_dot(u, wu_ref[...]), 0.0)
        return _dot((a * a).astype(BF16), wd_ref[...])

    groups = _row_groups(h_ref.shape[0])

    @pl.when(f == 0)
    def _():
        for rs in groups:
            u = _rms(h_ref[rs, :], gpre_ref[...]).astype(BF16)
            u_ref[rs, :] = u
            acc_ref[rs, :] = ffn(u)

    @pl.when(jnp.logical_and(f > 0, f < last))
    def _():
        acc_ref[...] += ffn(u_ref[...])

    @pl.when(f == last)
    def _():
        for rs in groups:
            total = acc_ref[rs, :] + ffn(u_ref[rs, :])
            o_ref[rs, :] = h_ref[rs, :] + _rms(total, gpost_ref[...])


def _mlp(h, g_pre, w_up, w_down, g_post, *, tm, tf):
    m, d = h.shape
    ff = w_up.shape[1]
    assert ff // tf >= 2, "the first and last hidden tiles take different branches"
    return pl.pallas_call(
        _mlp_kernel,
        grid=(m // tm, ff // tf),
        in_specs=[
            pl.BlockSpec((tm, d), lambda i, f: (i, 0)),
            pl.BlockSpec((1, d), lambda i, f: (0, 0)),
            pl.BlockSpec((d, tf), lambda i, f: (0, f)),
            pl.BlockSpec((tf, d), lambda i, f: (f, 0)),
            pl.BlockSpec((1, d), lambda i, f: (0, 0)),
        ],
        out_specs=pl.BlockSpec((tm, d), lambda i, f: (i, 0)),
        out_shape=jax.ShapeDtypeStruct((m, d), F32),
        scratch_shapes=[pltpu.VMEM((tm, d), BF16), pltpu.VMEM((tm, d), F32)],
        compiler_params=pltpu.CompilerParams(
            dimension_semantics=("parallel", "arbitrary"),
            vmem_limit_bytes=VMEM_LIMIT),
        name="mlp",
    )(h, g_pre, w_up, w_down, g_post)


def _ple_kernel(h_ref, p_ref, g_ref, wg_ref, wp_ref, o_ref):
    groups = _row_groups(h_ref.shape[0])
    parts = []
    for rs in groups:
        gate_pre = _dot(_rms(h_ref[rs, :], g_ref[...]).astype(BF16), wg_ref[...])
        parts.append((gate_pre, _dot(p_ref[rs, :].astype(BF16), wp_ref[...])))
    for rs, (gate_pre, e) in zip(groups, parts):
        o_ref[rs, :] = h_ref[rs, :] + _sigmoid(gate_pre) * e


def _ple(h, p2, gain, w_gate, w_proj, *, tm):
    m, d = h.shape
    return pl.pallas_call(
        _ple_kernel,
        grid=(m // tm,),
        in_specs=[
            pl.BlockSpec((tm, d), lambda i: (i, 0)),
            pl.BlockSpec((tm, p2.shape[1]), lambda i: (i, 0)),
            _resident(gain.shape),
            _resident(w_gate.shape),
            _resident(w_proj.shape),
        ],
        out_specs=pl.BlockSpec((tm, d), lambda i: (i, 0)),
        out_shape=jax.ShapeDtypeStruct((m, d), F32),
        compiler_params=pltpu.CompilerParams(
            dimension_semantics=("parallel",),
            vmem_limit_bytes=VMEM_LIMIT),
        name="ple",
    )(h, p2, gain, w_gate, w_proj)


def kernel(x, p, norm_mix_pre, norm_mix_post, w_in, w_gate_up, b_gate, gla_norm, w_branch_gla,
           w_branch_sb, w_out, norm_mlp_pre, norm_mlp_post, w_mlp_up, w_mlp_down, norm_ple,
           w_ple_gate, w_ple_proj):
    batch, seq, d = x.shape
    depth = w_in.shape[0]
    m = batch * seq
    sb_w = d // 2
    sb_heads = sb_w // SB_DH
    t = TILES
    h = x.reshape(m, d)
    row = lambda g: g.reshape(1, -1)
    for i in range(depth):
        lr0 = 2 * _KEY_W + _VAL_W
        w_t = jnp.swapaxes(w_in[i], 0, 1)
        w_a, w_lr = _head_cast(w_t, n_a=lr0, rank=GATE_RANK, block=t["head_cast_block"])
        w_gu = jnp.pad(w_gate_up[i], ((0, LANE - GATE_RANK), (0, 0))).astype(BF16)
        z_a, z_lr, u, w_b = _in_proj_a(h, row(norm_mix_pre[i]), w_a, w_lr, w_t,
                                       row_lo=lr0 + GATE_RANK, tm=t["in_proj_a_tm"])
        z_b = _in_proj_b(u, w_b, tm=t["in_proj_b_tm"], tn=t["in_proj_b_tn"])
        o_a = _gla(z_a, z_b, z_lr, w_gu, row(b_gate[i]), row(gla_norm[i]),
                   batch=batch, seq=seq, rows=t["gla_rows"])
        sq_off = _VAL_W // sb_w
        later_weights = (w_branch_gla[i], w_branch_sb[i], w_out[i], w_mlp_up[i], w_mlp_down[i],
                         w_ple_gate[i])
        o_b, (w_bg, w_bs, w_o, w_up, w_down, w_pg) = _sb_attn(
            z_b, later_weights, batch=batch, seq=seq, heads=sb_heads, q_off=sq_off,
            k_off=sq_off + 1, v_off=sq_off + 2, tq=t["sb_tq"], tk=t["sb_tk"])
        gate_blk = (_VAL_W + 3 * sb_w) // d
        h = _mix_out(o_a, o_b, z_b, h, w_bg, w_bs, w_o, row(norm_mix_post[i]),
                     ga_blk=gate_blk, gb_blk=gate_blk + 1, tm=t["mix_out_tm"])
        h = _mlp(h, row(norm_mlp_pre[i]), w_up, w_down, row(norm_mlp_post[i]),
                 tm=t["mlp_tm"], tf=t["mlp_tf"])
        h = _ple(h, p[i].reshape(m, -1), row(norm_ple[i]), w_pg, w_ple_proj[i].astype(BF16),
                 tm=t["ple_tm"])
    return h.reshape(batch, seq, d)
```
